```python
import math
import jax, jax.numpy as jnp
from jax import lax
import numpy as np

D_MODEL = 1024
BATCH = 4
SEQ = 8192
DEPTH = 4

HEAD_DIM = 64
A_HEADS = 4
B_HEADS = 4
C_HEADS = 4
A_WIDTH = A_HEADS * HEAD_DIM
HGRN_EXPAND = 64
B_KEY_WIDTH = B_HEADS * HGRN_EXPAND
B_WIDTH = B_HEADS * HEAD_DIM
C_QK_WIDTH = C_HEADS * 2 * HEAD_DIM
C_VDIM = 2 * HEAD_DIM
C_WIDTH = C_HEADS * C_VDIM
D_MIX = A_WIDTH + B_WIDTH + C_WIDTH
SPLIT_SIZES = (A_WIDTH, A_WIDTH, A_WIDTH,
               B_KEY_WIDTH, B_KEY_WIDTH, B_KEY_WIDTH,
               B_WIDTH, B_WIDTH,
               C_QK_WIDTH, C_QK_WIDTH, C_WIDTH)
IN_COLS = sum(SPLIT_SIZES)
ROPE_THETA = 500000.0
ROPE_DIM = HEAD_DIM // 4
DILATED_PAIRS = ((128, 1), (512, 4), (2048, 16))
SWA_BLOCK = 64
HGRN_CHUNK = 64
Q_BLOCK = 128
D_FF = 2816
CONV_WIDTH = 3
EPS = 1e-6
NEG_INF = -1e30

kernel_name = "hymba_style_dilated_hgrn2_diffattn_encoder"


def rms_norm(x, g):
    xf = x.astype(jnp.float32)
    y = xf * lax.rsqrt(jnp.mean(xf * xf, axis=-1, keepdims=True) + EPS)
    return (y * g.astype(jnp.float32)).astype(x.dtype)


def split_cols(h):
    outs, start = [], 0
    for size in SPLIT_SIZES:
        outs.append(h[..., start:start + size])
        start += size
    return outs


def rope_tables(seq):
    pos = jnp.arange(seq, dtype=jnp.float32)
    inv = ROPE_THETA ** (-jnp.arange(0, ROPE_DIM, 2, dtype=jnp.float32) / ROPE_DIM)
    ang = pos[:, None] * inv[None, :]
    return jnp.cos(ang), jnp.sin(ang)


def partial_rope(x, cos, sin):
    half = ROPE_DIM // 2
    x1, x2, rest = x[..., :half], x[..., half:ROPE_DIM], x[..., ROPE_DIM:]
    return jnp.concatenate([x1 * cos - x2 * sin, x2 * cos + x1 * sin, rest], axis=-1)


def heads(t, n):
    b, s, w = t.shape
    return t.reshape(b, s, n, w // n).transpose(0, 2, 1, 3)


def dilated_branch(q, k, v, window, dilation):
    b, h, s, hd = q.shape
    half = window // (2 * dilation)
    L = s // dilation
    nb = -(-L // SWA_BLOCK)
    Lp = nb * SWA_BLOCK

    def to_sub(t):
        return t.reshape(b, h, L, dilation, hd).transpose(0, 1, 3, 2, 4)

    qs = jnp.pad(to_sub(q), ((0, 0), (0, 0), (0, 0), (0, Lp - L), (0, 0)))
    pad_k = ((0, 0), (0, 0), (0, 0), (SWA_BLOCK, Lp - L + SWA_BLOCK), (0, 0))
    ks = jnp.pad(to_sub(k), pad_k)
    vs = jnp.pad(to_sub(v), pad_k)

    def windows(t):
        t = t.reshape(b, h, dilation, nb + 2, SWA_BLOCK, hd)
        return jnp.concatenate([t[:, :, :, 0:nb], t[:, :, :, 1:nb + 1], t[:, :, :, 2:nb + 2]], axis=-2)

    kw, vw = windows(ks), windows(vs)
    qb = qs.reshape(b, h, dilation, nb, SWA_BLOCK, hd)
    sc = jnp.einsum('bhrnqd,bhrnkd->bhrnqk', qb, kw) * (1.0 / math.sqrt(hd))
    blk = jnp.arange(nb)[:, None, None]
    qpos = blk * SWA_BLOCK + jnp.arange(SWA_BLOCK)[None, :, None]
    kpos = (blk - 1) * SWA_BLOCK + jnp.arange(3 * SWA_BLOCK)[None, None, :]
    mask = (jnp.abs(kpos - qpos) <= half) & (kpos >= 0) & (kpos < L)
    sc = jnp.where(mask, sc, NEG_INF)
    m = jnp.max(sc, axis=-1, keepdims=True)
    p = jnp.exp(sc - m)
    den = jnp.sum(p, axis=-1)
    o = jnp.einsum('bhrnqk,bhrnkd->bhrnqd', p, vw) / den[..., None]
    lse = m[..., 0] + jnp.log(den)
    o = o.reshape(b, h, dilation, Lp, hd)[:, :, :, :L].transpose(0, 1, 3, 2, 4).reshape(b, h, s, hd)
    lse = lse.reshape(b, h, dilation, Lp)[..., :L].transpose(0, 1, 3, 2).reshape(b, h, s)
    return o, lse


def dilated_mixer(q, k, v):
    outs, lses = [], []
    for window, dilation in DILATED_PAIRS:
        o, lse = dilated_branch(q, k, v, window, dilation)
        outs.append(o)
        lses.append(lse)
    w = jax.nn.softmax(jnp.stack(lses, axis=0), axis=0)
    return jnp.sum(w[..., None] * jnp.stack(outs, axis=0), axis=0)


def hgrn_scan(q, k, v, logf):
    b, h, s, dk = q.shape
    dv = v.shape[-1]
    nc = s // HGRN_CHUNK

    def chunks(t):
        return t.reshape(b, h, nc, HGRN_CHUNK, t.shape[-1]).transpose(2, 0, 1, 3, 4)

    tri = jnp.tril(jnp.ones((HGRN_CHUNK, HGRN_CHUNK), dtype=bool))[:, :, None]

    def step(state, inp):
        qc, kc, vc, gc = inp
        a = jnp.cumsum(gc, axis=-2)
        inter = jnp.einsum('bhtd,bhdv->bhtv', qc * jnp.exp(a), state)
        diff = a[:, :, :, None, :] - a[:, :, None, :, :]
        decay = jnp.exp(jnp.where(tri, diff, -jnp.inf))
        scores = jnp.einsum('bhtsd,bhsd->bhts', qc[:, :, :, None, :] * decay, kc)
        intra = jnp.einsum('bhts,bhsv->bhtv', scores, vc)
        a_last = a[:, :, -1:, :]
        new_state = (jnp.exp(a_last[:, :, 0, :])[..., None] * state
                     + jnp.einsum('bhsd,bhsv->bhdv', kc * jnp.exp(a_last - a), vc))
        return new_state, inter + intra

    init = jnp.zeros((b, h, dk, dv), jnp.float32)
    _, out = lax.scan(step, init, (chunks(q), chunks(k), chunks(v), chunks(logf)))
    return out.transpose(1, 2, 0, 3, 4).reshape(b, h, s, dv)


def hgrn_forget(z, lb):
    z = z.astype(jnp.float32)
    f = lb + (1.0 - lb) * jax.nn.sigmoid(z)
    return jnp.log(f), (1.0 - lb) * jax.nn.sigmoid(-z)


def diff_attention(q, k, v, lam):
    b, h, _, s, hd = q.shape
    nq = s // Q_BLOCK
    qb = q.reshape(b, h, 2, nq, Q_BLOCK, hd).transpose(3, 0, 1, 2, 4, 5)
    scale = 1.0 / math.sqrt(hd)

    def block(qblk):
        sc = jnp.einsum('bhcqd,bhckd->bhcqk', qblk, k) * scale
        a = jax.nn.softmax(sc.astype(jnp.float32), axis=-1)
        w = a[:, :, 0] - lam * a[:, :, 1]
        return jnp.einsum('bhqk,bhkv->bhqv', w, v)

    o = lax.map(block, qb)
    return o.transpose(1, 2, 0, 3, 4).reshape(b, h, s, v.shape[-1])


def mixer_layer(xn, w_in_l, w_out_l, lb_f, lb_b, hgrn_g, lam_p, diff_g, layer, cos, sin):
    b, s, _ = xn.shape
    (qa, ka, va, qb_, fzf, fzb, ib, gb, qc, kc, vc) = split_cols(xn @ w_in_l)

    qa = partial_rope(heads(qa, A_HEADS).astype(jnp.float32), cos, sin)
    ka = partial_rope(heads(ka, A_HEADS).astype(jnp.float32), cos, sin)
    oa = dilated_mixer(qa, ka, heads(va, A_HEADS).astype(jnp.float32))
    oa = oa.transpose(0, 2, 1, 3).reshape(b, s, A_WIDTH).astype(xn.dtype)

    logf_f, k_f = hgrn_forget(fzf, lb_f)
    logf_b, k_b = hgrn_forget(fzb, lb_b)
    qh = heads(qb_, B_HEADS).astype(jnp.float32)
    vh = heads(ib, B_HEADS).astype(jnp.float32)
    o_fwd = hgrn_scan(qh, heads(k_f, B_HEADS), vh, heads(logf_f, B_HEADS))
    flip = lambda t: jnp.flip(t, axis=2)
    o_bwd = flip(hgrn_scan(flip(qh), flip(heads(k_b, B_HEADS)), flip(vh), flip(heads(logf_b, B_HEADS))))
    ob = (o_fwd + o_bwd).transpose(0, 2, 1, 3)
    ob = rms_norm(ob, hgrn_g) * jax.nn.silu(gb.astype(jnp.float32)).reshape(b, s, B_HEADS, HEAD_DIM)
    ob = ob.reshape(b, s, B_WIDTH).astype(xn.dtype)

    lam_init = 0.8 - 0.6 * math.exp(-0.3 * layer)
    lp = lam_p.astype(jnp.float32)
    lam = jnp.exp(jnp.sum(lp[0] * lp[1])) - jnp.exp(jnp.sum(lp[2] * lp[3])) + lam_init
    qc = qc.reshape(b, s, C_HEADS, 2, HEAD_DIM).transpose(0, 2, 3, 1, 4).astype(jnp.float32)
    kc = kc.reshape(b, s, C_HEADS, 2, HEAD_DIM).transpose(0, 2, 3, 1, 4).astype(jnp.float32)
    qc = partial_rope(qc, cos, sin)
    kc = partial_rope(kc, cos, sin)
    oc = diff_attention(qc, kc, heads(vc, C_HEADS).astype(jnp.float32), lam)
    oc = rms_norm(oc.transpose(0, 2, 1, 3), diff_g) * (1.0 - lam_init)
    oc = oc.reshape(b, s, C_WIDTH).astype(xn.dtype)

    return jnp.concatenate([oa, ob, oc], axis=-1) @ w_out_l


def conv_ffn(xn, w_up_l, cw, cb, w_down_l):
    u = xn @ w_up_l
    up = jnp.pad(u, ((0, 0), (1, 1), (0, 0)))
    u = up[:, :-2] * cw[0] + up[:, 1:-1] * cw[1] + up[:, 2:] * cw[2] + cb
    gate, val = jnp.split(u, 2, axis=-1)
    return (jax.nn.silu(gate) * val) @ w_down_l


def setup_inputs(seed: int = 0) -> dict:
    key = jax.random.key(seed)
    ks = jax.random.split(key, 16)
    f32 = jnp.float32
    nrm = lambda k, shape, scale: jax.random.normal(k, shape, f32) * scale
    gain = lambda k, shape: 1.0 + 0.05 * jax.random.normal(k, shape, f32)
    return {
        "x": jax.random.normal(ks[0], (BATCH, SEQ, D_MODEL), f32),
        "w_in": nrm(ks[1], (DEPTH, D_MODEL, IN_COLS), D_MODEL ** -0.5),
        "w_out": nrm(ks[2], (DEPTH, D_MIX, D_MODEL), D_MIX ** -0.5),
        "lb_logits": nrm(ks[3], (2, DEPTH, B_KEY_WIDTH), 0.1),
        "hgrn_norm": gain(ks[4], (DEPTH, HEAD_DIM)),
        "diff_lambda": nrm(ks[5], (DEPTH, 4, HEAD_DIM), 0.1),
        "diff_norm": gain(ks[6], (DEPTH, C_VDIM)),
        "w_up": nrm(ks[7], (DEPTH, D_MODEL, 2 * D_FF), D_MODEL ** -0.5),
        "conv_w": nrm(ks[8], (DEPTH, CONV_WIDTH, 2 * D_FF), CONV_WIDTH ** -0.5),
        "conv_b": nrm(ks[9], (DEPTH, 2 * D_FF), 0.02),
        "w_down": nrm(ks[10], (DEPTH, D_FF, D_MODEL), D_FF ** -0.5),
        "norm_pre_mix": gain(ks[11], (DEPTH, D_MODEL)),
        "norm_post_mix": gain(ks[12], (DEPTH, D_MODEL)),
        "norm_pre_ffn": gain(ks[13], (DEPTH, D_MODEL)),
        "norm_post_ffn": gain(ks[14], (DEPTH, D_MODEL)),
    }


def reference(x, w_in, w_out, lb_logits, hgrn_norm, diff_lambda, diff_norm, w_up, conv_w, conv_b,
              w_down, norm_pre_mix, norm_post_mix, norm_pre_ffn, norm_post_ffn):
    cos, sin = rope_tables(x.shape[1])
    p = jax.nn.softmax(lb_logits.astype(jnp.float32), axis=1)
    lbs = jnp.cumsum(p, axis=1) - p[:, :1]
    for l in range(DEPTH):
        xn = rms_norm(x, norm_pre_mix[l])
        mix = mixer_layer(xn, w_in[l], w_out[l], lbs[0, l], lbs[1, l], hgrn_norm[l],
                          diff_lambda[l], diff_norm[l], l, cos, sin)
        x = x + rms_norm(mix, norm_post_mix[l])
        xn = rms_norm(x, norm_pre_ffn[l])
        ff = conv_ffn(xn, w_up[l], conv_w[l], conv_b[l], w_down[l])
        x = x + rms_norm(ff, norm_post_ffn[l])
    return x
```

```python
import functools
import math

import jax
import jax.numpy as jnp
from jax import lax
from jax.experimental import pallas as pl
from jax.experimental.pallas import tpu as pltpu

F32 = jnp.float32
BF16 = jnp.bfloat16

HEAD_DIM = 64
N_HEADS = 4
A_WIDTH = N_HEADS * HEAD_DIM
B_WIDTH = N_HEADS * HEAD_DIM
C_WIDTH = N_HEADS * 2 * HEAD_DIM
ROPE_THETA = 500000.0
ROPE_DIM = HEAD_DIM // 4
ROPE_HALF = ROPE_DIM // 2
DILATIONS = (1, 4, 16)
BAND_HALF = 64
HGRN_CHUNK = 64
CONV_WIDTH = 3
EPS = 1e-6
NEG_INF = -1e30
EXP_CLAMP = 80.0

LANES = 128
BF16_ROWS = 16
VMEM_LIMIT = 56 * 1024 * 1024

OFF_QA, OFF_KA, OFF_VA = 0, 256, 512
OFF_QB, OFF_ZF, OFF_ZB, OFF_IB, OFF_GB = 768, 1024, 1280, 1536, 1792
OFF_QC, OFF_KC, OFF_VC = 2048, 2560, 3072


def _cparams(sem):
    return pltpu.CompilerParams(dimension_semantics=sem, vmem_limit_bytes=VMEM_LIMIT)


def _rms(x, g):
    ms = jnp.mean(x * x, axis=-1, keepdims=True)
    return x * lax.rsqrt(ms + EPS) * g


def _dot(a, b):
    return jnp.dot(a, b, preferred_element_type=F32)


def _dot_nt(a, b):
    return lax.dot_general(a, b, (((1,), (1,)), ((), ())), preferred_element_type=F32)


def _dot_tn(a, b):
    return lax.dot_general(a, b, (((0,), (0,)), ((), ())), preferred_element_type=F32)


def _in_proj_kernel(x_ref, g_ref, w_ref, rc_ref, rs1_ref, rs2_ref,
                    qa_ref, ka_ref, va_ref, qb_ref, zf_ref, zb_ref, ib_ref, gb_ref,
                    qct_ref, kc_ref, vct_ref):
    xn = _rms(x_ref[...], g_ref[...]).astype(BF16)
    rc, rs1, rs2 = rc_ref[...], rs1_ref[...], rs2_ref[...]

    def proj(c0, n):
        return _dot(xn, w_ref[:, c0:c0 + n])

    def rope(a):
        outs = []
        for j in range(a.shape[1] // LANES):
            blk = a[:, j * LANES:(j + 1) * LANES]
            outs.append(blk * rc + pltpu.roll(blk, LANES - ROPE_HALF, 1) * rs1
                        + pltpu.roll(blk, ROPE_HALF, 1) * rs2)
        return outs

    def cat(blks):
        return jnp.concatenate(blks, axis=1)

    scale = 1.0 / math.sqrt(HEAD_DIM)
    qa_ref[...] = (cat(rope(proj(OFF_QA, 256))) * scale).astype(BF16)
    ka_ref[...] = cat(rope(proj(OFF_KA, 256))).astype(BF16)
    va_ref[...] = proj(OFF_VA, 256).astype(BF16)
    qb_ref[...] = proj(OFF_QB, 256).astype(BF16)
    zf_ref[...] = proj(OFF_ZF, 256)
    zb_ref[...] = proj(OFF_ZB, 256)
    ib_ref[...] = proj(OFF_IB, 256).astype(BF16)
    gb_ref[...] = proj(OFF_GB, 256)
    for half in range(2):
        qblks = rope(proj(OFF_QC + 256 * half, 256))
        kblks = rope(proj(OFF_KC + 256 * half, 256))
        v = proj(OFF_VC + 256 * half, 256)
        kc_ref[:, 256 * half:256 * (half + 1)] = cat(kblks).astype(BF16)
        for j in range(2):
            r0 = 256 * half + LANES * j
            qct_ref[0, r0:r0 + LANES, :] = (qblks[j] * scale).T.astype(BF16)
            vct_ref[0, r0:r0 + LANES, :] = v[:, j * LANES:(j + 1) * LANES].T.astype(BF16)


def _in_proj(x2, g, w, rc, rs1, rs2, batch, seq, tm):
    t, d = x2.shape
    n_cols = w.shape[1]
    tiles_per_seq = seq // tm
    row = lambda i: (i, 0)
    const = lambda i: (0, 0)
    ropei = lambda i: (i % tiles_per_seq, 0)
    tr = lambda i: (i // tiles_per_seq, 0, i % tiles_per_seq)
    nat = lambda n, dt: jax.ShapeDtypeStruct((t, n), dt)
    out_shape = [nat(256, BF16), nat(256, BF16), nat(256, BF16),
                 nat(256, BF16), nat(256, F32), nat(256, F32), nat(256, BF16), nat(256, F32),
                 jax.ShapeDtypeStruct((batch, C_WIDTH, seq), BF16),
                 nat(C_WIDTH, BF16),
                 jax.ShapeDtypeStruct((batch, C_WIDTH, seq), BF16)]
    out_specs = [pl.BlockSpec((tm, 256), row)] * 8 + [
        pl.BlockSpec((1, C_WIDTH, tm), tr), pl.BlockSpec((tm, C_WIDTH), row),
        pl.BlockSpec((1, C_WIDTH, tm), tr)]
    return pl.pallas_call(
        _in_proj_kernel,
        grid=(t // tm,),
        in_specs=[pl.BlockSpec((tm, d), row), pl.BlockSpec((1, d), const),
                  pl.BlockSpec((d, n_cols), const),
                  pl.BlockSpec((tm, LANES), ropei), pl.BlockSpec((tm, LANES), ropei),
                  pl.BlockSpec((tm, LANES), ropei)],
        out_specs=out_specs,
        out_shape=out_shape,
        compiler_params=_cparams(("parallel",)),
        name="in_proj",
    )(x2, g, w, rc, rs1, rs2)


def _banded_kernel(q_ref, k_ref, v_ref, o_ref, lse_ref, *, seq_len, q_block, sub, k_win):
    i = pl.program_id(1)
    lane = lax.broadcasted_iota(jnp.int32, (sub, A_WIDTH), 1)
    qi = lax.broadcasted_iota(jnp.int32, (sub, k_win), 0)
    ki = lax.broadcasted_iota(jnp.int32, (sub, k_win), 1)
    for sb in range(q_block // sub):
        q0 = i * q_block + sb * sub
        ks = pl.multiple_of(jnp.clip(q0 - BAND_HALF, 0, seq_len - k_win), BAND_HALF)
        kwin = k_ref[0, pl.ds(ks, k_win), :]
        vwin = v_ref[0, pl.ds(ks, k_win), :]
        qblk = q_ref[0, sb * sub:(sb + 1) * sub, :]
        band = jnp.abs((ks + ki) - (q0 + qi)) <= BAND_HALF
        o_acc = jnp.zeros((sub, A_WIDTH), F32)
        lse_acc = jnp.zeros((sub, A_WIDTH), F32)
        for h in range(N_HEADS):
            in_head = (lane >= h * HEAD_DIM) & (lane < (h + 1) * HEAD_DIM)
            qh = jnp.where(in_head, qblk, jnp.zeros_like(qblk))
            s = jnp.where(band, _dot_nt(qh, kwin), NEG_INF)
            m = jnp.max(s, axis=-1, keepdims=True)
            p = jnp.exp(s - m)
            den = jnp.sum(p, axis=-1, keepdims=True)
            o = _dot(p.astype(BF16), vwin) / den
            o_acc = jnp.where(in_head, o, o_acc)
            lse_acc = jnp.where(in_head, m + jnp.log(den), lse_acc)
        o_ref[0, sb * sub:(sb + 1) * sub, :] = o_acc
        lse_ref[0, sb * sub:(sb + 1) * sub, :] = lse_acc


def _banded(q, k, v):
    n_seq, seq_len, w = q.shape
    q_block = min(512, seq_len)
    sub = min(128, seq_len)
    k_win = min(sub + 2 * BAND_HALF, seq_len)
    kern = functools.partial(_banded_kernel, seq_len=seq_len, q_block=q_block, sub=sub, k_win=k_win)
    full = pl.BlockSpec((1, seq_len, w), lambda s, i: (s, 0, 0))
    blk = pl.BlockSpec((1, q_block, w), lambda s, i: (s, i, 0))
    return pl.pallas_call(
        kern,
        grid=(n_seq, seq_len // q_block),
        in_specs=[blk, full, full],
        out_specs=[blk, blk],
        out_shape=[jax.ShapeDtypeStruct((n_seq, seq_len, w), F32)] * 2,
        compiler_params=_cparams(("parallel", "arbitrary")),
        name="banded_attn",
    )(q, k, v)


def _to_sub(t, batch, seq, dil):
    w = t.shape[-1]
    return t.reshape(batch, seq // dil, dil, w).transpose(0, 2, 1, 3).reshape(batch * dil, seq // dil, w)


def _from_sub(t, batch, seq, dil):
    w = t.shape[-1]
    return t.reshape(batch, dil, seq // dil, w).transpose(0, 2, 1, 3).reshape(batch * seq, w)


def _split3(x):
    hi = x.astype(BF16)
    r1 = x - hi.astype(F32)
    mid = r1.astype(BF16)
    lo = (r1 - mid.astype(F32)).astype(BF16)
    return hi, mid, lo


def _hgrn_kernel(lbl_ref, qf_ref, zf_ref, vf_ref, qb_ref, zb_ref, vb_ref,
                 of_ref, ob_ref, sf_ref, sb_ref, *, depth, layer, n_chunks):
    c = HGRN_CHUNK

    @pl.when(pl.program_id(2) == 0)
    def _():
        sf_ref[...] = jnp.zeros_like(sf_ref)
        sb_ref[...] = jnp.zeros_like(sb_ref)

    def lower_bound(logits):
        e = jnp.exp(logits - jnp.max(logits, axis=0, keepdims=True))
        p = e / jnp.sum(e, axis=0, keepdims=True)
        lb = jnp.zeros((1, LANES), F32)
        for j in range(1, layer + 1):
            lb = lb + p[j:j + 1, :]
        return lb

    lb_f = lower_bound(lbl_ref[0:depth, :])
    lb_b = lower_bound(lbl_ref[depth:2 * depth, :])

    ti = lax.broadcasted_iota(jnp.int32, (c, c), 0)
    si = lax.broadcasted_iota(jnp.int32, (c, c), 1)
    tril = si <= ti
    triu = si >= ti
    ltri = jnp.where(tril, 1.0, 0.0).astype(BF16)
    lane = lax.broadcasted_iota(jnp.int32, (c, LANES), 1)
    head0 = lane < HEAD_DIM
    er = lax.broadcasted_iota(jnp.int32, (LANES, LANES), 0) < HEAD_DIM
    ec = lax.broadcasted_iota(jnp.int32, (LANES, LANES), 1) < HEAD_DIM
    same_head = er == ec

    def gates(z, lb):
        f = lb + (1.0 - lb) * jax.nn.sigmoid(z)
        return jnp.log(f), (1.0 - lb) * jax.nn.sigmoid(-z)

    def cumsum(logf):
        hi, mid, lo = _split3(logf)
        return _dot(ltri, hi) + _dot(ltri, mid) + _dot(ltri, lo)

    def intra(qd, kd, v, tri):
        kd = kd.astype(BF16)
        zero = jnp.zeros_like(qd)
        s0 = jnp.where(tri, _dot_nt(jnp.where(head0, qd, zero).astype(BF16), kd), 0.0)
        s1 = jnp.where(tri, _dot_nt(jnp.where(head0, zero, qd).astype(BF16), kd), 0.0)
        return jnp.where(head0, _dot(s0.astype(BF16), v), _dot(s1.astype(BF16), v))

    def step(q, z, v, lb, st_ref, forward):
        logf, kk = gates(z, lb)
        a = cumsum(logf)
        a_last = a[c - 1:c, :]
        if forward:
            e = a
            mid = a[c // 2 - 1:c // 2, :]
            q_in = q * jnp.exp(a)
            k_st = kk * jnp.exp(a_last - a)
            tri = tril
        else:
            e = a - logf
            mid = e[c // 2:c // 2 + 1, :]
            q_in = q * jnp.exp(a_last - e)
            k_st = kk * jnp.exp(e)
            tri = triu
        sgn = 1.0 if forward else -1.0
        qd = q * jnp.exp(jnp.minimum(sgn * (e - mid), EXP_CLAMP))
        kd = kk * jnp.exp(jnp.minimum(sgn * (mid - e), EXP_CLAMP))
        st = st_ref[...]
        o = _dot_nt(q_in.astype(BF16), st.astype(BF16)) + intra(qd, kd, v, tri)
        ut = _dot_tn(v, k_st.astype(BF16))
        st_ref[...] = st * jnp.exp(a_last) + jnp.where(same_head, ut, 0.0)
        return o

    for j in range(n_chunks):
        r = slice(j * c, (j + 1) * c)
        of_ref[r, :] = step(qf_ref[r, :].astype(F32), zf_ref[r, :], vf_ref[r, :], lb_f, sf_ref, True)
    for j in reversed(range(n_chunks)):
        r = slice(j * c, (j + 1) * c)
        ob_ref[r, :] = step(qb_ref[r, :].astype(F32), zb_ref[r, :], vb_ref[r, :], lb_b, sb_ref, False)


def _hgrn(lb_logits2, q, zf, zb, v, batch, seq, layer, rows):
    t = q.shape[0]
    depth = lb_logits2.shape[0] // 2
    n = seq // rows
    fwd = lambda b, p, i: (b * n + i, p)
    bwd = lambda b, p, i: (b * n + (n - 1 - i), p)
    kern = functools.partial(_hgrn_kernel, depth=depth, layer=layer, n_chunks=rows // HGRN_CHUNK)
    spec = lambda im: pl.BlockSpec((rows, LANES), im)
    return pl.pallas_call(
        kern,
        grid=(batch, B_WIDTH // LANES, n),
        in_specs=[pl.BlockSpec((2 * depth, LANES), lambda b, p, i: (0, p)),
                  spec(fwd), spec(fwd), spec(fwd), spec(bwd), spec(bwd), spec(bwd)],
        out_specs=[spec(fwd), spec(bwd)],
        out_shape=[jax.ShapeDtypeStruct((t, B_WIDTH), F32)] * 2,
        scratch_shapes=[pltpu.VMEM((LANES, LANES), F32), pltpu.VMEM((LANES, LANES), F32)],
        compiler_params=_cparams(("parallel", "parallel", "arbitrary")),
        name="hgrn2",
    )(lb_logits2, q, zf, v, q, zb, v)


def _diff_kernel(lam_ref, g_ref, qt_ref, k_ref, vt_ref, o_ref, *, seq, tq, kc, lam_init):
    lp = lam_ref[...]
    lam = (jnp.exp(jnp.sum(lp[0:1, :] * lp[1:2, :], axis=1, keepdims=True))
           - jnp.exp(jnp.sum(lp[2:3, :] * lp[3:4, :], axis=1, keepdims=True)) + lam_init)
    qt = qt_ref[0]
    row = lax.broadcasted_iota(jnp.int32, qt.shape, 0)
    zero = jnp.zeros_like(qt)
    q1 = jnp.where(row < HEAD_DIM, qt, zero)
    q2 = jnp.where(row < HEAD_DIM, zero, qt)

    def update(s, vt, m, l, acc):
        m_new = jnp.maximum(m, jnp.max(s, axis=0, keepdims=True))
        alpha = jnp.exp(m - m_new)
        p = jnp.exp(s - m_new)
        l = l * alpha + jnp.sum(p, axis=0, keepdims=True)
        acc = acc * alpha + _dot(vt, p.astype(BF16))
        return m_new, l, acc

    def body(j, carry):
        m1, l1, a1, m2, l2, a2 = carry
        k0 = pl.multiple_of(j * kc, kc)
        k = k_ref[pl.ds(k0, kc), :]
        vt = vt_ref[0, :, pl.ds(k0, kc)]
        m1, l1, a1 = update(_dot(k, q1), vt, m1, l1, a1)
        m2, l2, a2 = update(_dot(k, q2), vt, m2, l2, a2)
        return m1, l1, a1, m2, l2, a2

    vec = lambda val: jnp.full((1, tq), val, F32)
    mat = jnp.zeros((2 * HEAD_DIM, tq), F32)
    m1, l1, a1, m2, l2, a2 = lax.fori_loop(
        0, seq // kc, body, (vec(NEG_INF), vec(0.0), mat, vec(NEG_INF), vec(0.0), mat))
    o = a1 / l1 - lam * (a2 / l2)
    ms = jnp.mean(o * o, axis=0, keepdims=True)
    y = o * lax.rsqrt(ms + EPS) * g_ref[...] * (1.0 - lam_init)
    o_ref[...] = y.T.astype(o_ref.dtype)


def _diff_attn(lam_p, g_col, qct, kc_nat, vct, batch, seq, layer, tq, kc):
    t = kc_nat.shape[0]
    lam_init = 0.8 - 0.6 * math.exp(-0.3 * layer)
    nq = seq // tq
    kern = functools.partial(_diff_kernel, seq=seq, tq=tq, kc=kc, lam_init=lam_init)
    return pl.pallas_call(
        kern,
        grid=(batch, N_HEADS, nq),
        in_specs=[pl.BlockSpec(lam_p.shape, lambda b, h, i: (0, 0)),
                  pl.BlockSpec((2 * HEAD_DIM, 1), lambda b, h, i: (0, 0)),
                  pl.BlockSpec((1, 2 * HEAD_DIM, tq), lambda b, h, i: (b, h, i)),
                  pl.BlockSpec((seq, 2 * HEAD_DIM), lambda b, h, i: (b, h)),
                  pl.BlockSpec((1, 2 * HEAD_DIM, seq), lambda b, h, i: (b, h, 0))],
        out_specs=pl.BlockSpec((tq, 2 * HEAD_DIM), lambda b, h, i: (b * nq + i, h)),
        out_shape=jax.ShapeDtypeStruct((t, C_WIDTH), BF16),
        compiler_params=_cparams(("parallel", "parallel", "arbitrary")),
        name="diff_attn",
    )(lam_p, g_col, qct, kc_nat, vct)


def _out_proj_kernel(x_ref, o1_ref, l1_ref, o2_ref, l2_ref, o3_ref, l3_ref,
                     of_ref, ob_ref, gb_ref, oc_ref, hg_ref, w_ref, g_ref, h_ref):
    l1, l2, l3 = l1_ref[...], l2_ref[...], l3_ref[...]
    mx = jnp.maximum(jnp.maximum(l1, l2), l3)
    w1, w2, w3 = jnp.exp(l1 - mx), jnp.exp(l2 - mx), jnp.exp(l3 - mx)
    oa = (w1 * o1_ref[...] + w2 * o2_ref[...] + w3 * o3_ref[...]) / (w1 + w2 + w3)

    y = of_ref[...] + ob_ref[...]
    y2 = y * y
    lane = lax.broadcasted_iota(jnp.int32, y.shape, 1)
    ms = jnp.zeros_like(y)
    for h in range(N_HEADS):
        in_head = (lane >= h * HEAD_DIM) & (lane < (h + 1) * HEAD_DIM)
        ms_h = jnp.sum(jnp.where(in_head, y2, 0.0), axis=-1, keepdims=True) * (1.0 / HEAD_DIM)
        ms = jnp.where(in_head, ms_h, ms)
    gb = gb_ref[...]
    ob = y * lax.rsqrt(ms + EPS) * hg_ref[...] * (gb * jax.nn.sigmoid(gb))

    cat = jnp.concatenate([oa.astype(BF16), ob.astype(BF16), oc_ref[...]], axis=1)
    mix = _dot(cat, w_ref[...])
    h_ref[...] = x_ref[...] + _rms(mix, g_ref[...])


def _out_proj(x2, a_outs, of, ob, gb, oc, hg, w, g, tm):
    t, d = x2.shape
    row = lambda i: (i, 0)
    const = lambda i: (0, 0)
    r256 = pl.BlockSpec((tm, 256), row)
    return pl.pallas_call(
        _out_proj_kernel,
        grid=(t // tm,),
        in_specs=[pl.BlockSpec((tm, d), row)] + [r256] * 9 + [
            pl.BlockSpec((tm, C_WIDTH), row), pl.BlockSpec((1, B_WIDTH), const),
            pl.BlockSpec(w.shape, const), pl.BlockSpec((1, d), const)],
        out_specs=pl.BlockSpec((tm, d), row),
        out_shape=jax.ShapeDtypeStruct((t, d), F32),
        compiler_params=_cparams(("parallel",)),
        name="out_proj",
    )(x2, *a_outs, of, ob, gb, oc, hg, w, g)


def _ffn_kernel(hp_ref, h_ref, hn_ref, gpre_ref, wup_ref, cw_ref, cb_ref, wdn_ref, gpost_ref,
                o_ref, xe_ref, act_ref, *, tm, tiles_per_seq, d_ff, cn):
    i = pl.program_id(0)
    halo = BF16_ROWS
    first = (i % tiles_per_seq) == 0
    last = (i % tiles_per_seq) == tiles_per_seq - 1
    gpre = gpre_ref[...]
    h = h_ref[...]
    xe_ref[halo:halo + tm, :] = _rms(h, gpre).astype(BF16)
    xe_ref[0:halo, :] = jnp.where(first, 0.0, _rms(hp_ref[...], gpre)).astype(BF16)
    xe_ref[halo + tm:2 * halo + tm, :] = jnp.where(last, 0.0, _rms(hn_ref[...], gpre)).astype(BF16)
    xe = xe_ref[...]

    def conv(c0):
        u = _dot(xe, wup_ref[:, c0:c0 + cn])
        cw = cw_ref[:, c0:c0 + cn]
        return (u[halo - 1:halo - 1 + tm] * cw[0:1] + u[halo:halo + tm] * cw[1:2]
                + u[halo + 1:halo + 1 + tm] * cw[2:3] + cb_ref[:, c0:c0 + cn])

    for c in range(d_ff // cn):
        gate = conv(c * cn)
        val = conv(d_ff + c * cn)
        act_ref[:, c * cn:(c + 1) * cn] = (gate * jax.nn.sigmoid(gate) * val).astype(BF16)
    ff = _dot(act_ref[...], wdn_ref[...])
    o_ref[...] = h + _rms(ff, gpost_ref[...])


def _ffn(h2, gpre, wup, cw, cb, wdn, gpost, seq, tm, cn):
    t, d = h2.shape
    d_ff = wdn.shape[0]
    halo = BF16_ROWS
    tiles_per_seq = seq // tm
    hb = tm // halo
    const = lambda i: (0, 0)
    kern = functools.partial(_ffn_kernel, tm=tm, tiles_per_seq=tiles_per_seq, d_ff=d_ff, cn=cn)
    return pl.pallas_call(
        kern,
        grid=(t // tm,),
        in_specs=[pl.BlockSpec((halo, d), lambda i: (jnp.maximum(i * hb - 1, 0), 0)),
                  pl.BlockSpec((tm, d), lambda i: (i, 0)),
                  pl.BlockSpec((halo, d), lambda i: (jnp.minimum((i + 1) * hb, t // halo - 1), 0)),
                  pl.BlockSpec((1, d), const),
                  pl.BlockSpec(wup.shape, const), pl.BlockSpec(cw.shape, const),
                  pl.BlockSpec(cb.shape, const), pl.BlockSpec(wdn.shape, const),
                  pl.BlockSpec((1, d), const)],
        out_specs=pl.BlockSpec((tm, d), lambda i: (i, 0)),
        out_shape=jax.ShapeDtypeStruct((t, d), F32),
        scratch_shapes=[pltpu.VMEM((tm + 2 * halo, d), BF16), pltpu.VMEM((tm, d_ff), BF16)],
        compiler_params=_cparams(("parallel",)),
        name="conv_ffn",
    )(h2, h2, h2, gpre, wup, cw, cb, wdn, gpost)


def _rope_tables(seq):
    pos = jnp.arange(seq, dtype=F32)
    inv = ROPE_THETA ** (-jnp.arange(0, ROPE_DIM, 2, dtype=F32) / ROPE_DIM)
    ang = pos[:, None] * inv[None, :]
    cos, sin = jnp.cos(ang), jnp.sin(ang)
    rest = HEAD_DIM - ROPE_DIM
    one, zero = jnp.ones((seq, rest), F32), jnp.zeros((seq, rest), F32)
    zh = jnp.zeros((seq, ROPE_HALF), F32)
    rep = LANES // HEAD_DIM
    rc = jnp.tile(jnp.concatenate([cos, cos, one], axis=1), (1, rep))
    rs1 = jnp.tile(jnp.concatenate([-sin, zh, zero], axis=1), (1, rep))
    rs2 = jnp.tile(jnp.concatenate([zh, sin, zero], axis=1), (1, rep))
    return rc, rs1, rs2


def kernel(x, w_in, w_out, lb_logits, hgrn_norm, diff_lambda, diff_norm, w_up, conv_w, conv_b,
           w_down, norm_pre_mix, norm_post_mix, norm_pre_ffn, norm_post_ffn):
    batch, seq, d = x.shape
    depth = w_in.shape[0]
    tm = min(512, seq)
    rc, rs1, rs2 = _rope_tables(seq)
    lbl2 = lb_logits.astype(F32).reshape(2 * depth, B_WIDTH)
    x2 = x.reshape(batch * seq, d)
    for l in range(depth):
        (qa, ka, va, qb, zf, zb, ib, gb, qct, kc, vct) = _in_proj(
            x2, norm_pre_mix[l].reshape(1, d), w_in[l].astype(BF16), rc, rs1, rs2, batch, seq, tm)

        a_outs = []
        for dil in DILATIONS:
            o, lse = _banded(_to_sub(qa, batch, seq, dil), _to_sub(ka, batch, seq, dil),
                             _to_sub(va, batch, seq, dil))
            a_outs += [_from_sub(o, batch, seq, dil), _from_sub(lse, batch, seq, dil)]

        of, ob = _hgrn(lbl2, qb, zf, zb, ib, batch, seq, l, rows=min(512, seq))

        oc = _diff_attn(diff_lambda[l].astype(F32), diff_norm[l].astype(F32).reshape(2 * HEAD_DIM, 1),
                        qct, kc, vct, batch, seq, l, tq=min(256, seq), kc=min(512, seq))

        hg = jnp.tile(hgrn_norm[l].astype(F32), N_HEADS).reshape(1, B_WIDTH)
        h2 = _out_proj(x2, a_outs, of, ob, gb, oc, hg, w_out[l].astype(BF16),
                       norm_post_mix[l].reshape(1, d), tm)

        x2 = _ffn(h2, norm_pre_ffn[l].reshape(1, d), w_up[l].astype(BF16), conv_w[l],
                  conv_b[l].reshape(1, -1), w_down[l].astype(BF16), norm_post_ffn[l].reshape(1, d),
                  seq, tm, cn=256)
    return x2.reshape(batch, seq, d)
```

```python
import functools
import math

import jax
import jax.numpy as jnp
from jax import lax
from jax.experimental import pallas as pl
from jax.experimental.pallas import tpu as pltpu

F32 = jnp.float32
BF16 = jnp.bfloat16

HEAD_DIM = 64
N_HEADS = 4
A_WIDTH = N_HEADS * HEAD_DIM
B_WIDTH = N_HEADS * HEAD_DIM
C_WIDTH = N_HEADS * 2 * HEAD_DIM
ROPE_THETA = 500000.0
ROPE_DIM = HEAD_DIM // 4
ROPE_HALF = ROPE_DIM // 2
DILATIONS = (1, 4, 16)
BAND_HALF = 64
HGRN_CHUNK = 64
CONV_WIDTH = 3
EPS = 1e-6
NEG_INF = -1e30
LOG2E = math.log2(math.e)
EXP_CLAMP = 80.0

LANES = 128
BF16_ROWS = 16
VMEM_LIMIT = 56 * 1024 * 1024

OFF_QA, OFF_KA, OFF_VA = 0, 256, 512
OFF_QB, OFF_ZF, OFF_ZB, OFF_IB, OFF_GB = 768, 1024, 1280, 1536, 1792
OFF_QC, OFF_KC, OFF_VC = 2048, 2560, 3072


def _cparams(sem):
    return pltpu.CompilerParams(dimension_semantics=sem, vmem_limit_bytes=VMEM_LIMIT)


def _rms(x, g):
    ms = jnp.mean(x * x, axis=-1, keepdims=True)
    return x * lax.rsqrt(ms + EPS) * g


def _dot(a, b):
    return jnp.dot(a, b, preferred_element_type=F32)


def _dot_nt(a, b):
    return lax.dot_general(a, b, (((1,), (1,)), ((), ())), preferred_element_type=F32)


def _dot_tn(a, b):
    return lax.dot_general(a, b, (((0,), (0,)), ((), ())), preferred_element_type=F32)


def _in_proj_kernel(x_ref, g_ref, w_ref, rc_ref, rs1_ref, rs2_ref,
                    qa_ref, ka_ref, va_ref, qb_ref, zf_ref, zb_ref, ib_ref, gb_ref,
                    qct_ref, kc_ref, vct_ref):
    xn = _rms(x_ref[...], g_ref[...]).astype(BF16)
    rc, rs1, rs2 = rc_ref[...], rs1_ref[...], rs2_ref[...]

    def proj(c0, n):
        return _dot(xn, w_ref[:, c0:c0 + n])

    def rope(a):
        outs = []
        for j in range(a.shape[1] // LANES):
            blk = a[:, j * LANES:(j + 1) * LANES]
            outs.append(blk * rc + pltpu.roll(blk, LANES - ROPE_HALF, 1) * rs1
                        + pltpu.roll(blk, ROPE_HALF, 1) * rs2)
        return outs

    def cat(blks):
        return jnp.concatenate(blks, axis=1)

    scale = 1.0 / math.sqrt(HEAD_DIM)
    qa_ref[...] = (cat(rope(proj(OFF_QA, 256))) * scale).astype(BF16)
    ka_ref[...] = cat(rope(proj(OFF_KA, 256))).astype(BF16)
    va_ref[...] = proj(OFF_VA, 256).astype(BF16)
    qb_ref[...] = proj(OFF_QB, 256).astype(BF16)
    zf_ref[...] = proj(OFF_ZF, 256)
    zb_ref[...] = proj(OFF_ZB, 256)
    ib_ref[...] = proj(OFF_IB, 256).astype(BF16)
    gb_ref[...] = proj(OFF_GB, 256)
    for half in range(2):
        qblks = rope(proj(OFF_QC + 256 * half, 256))
        kblks = rope(proj(OFF_KC + 256 * half, 256))
        v = proj(OFF_VC + 256 * half, 256)
        kc_ref[:, 256 * half:256 * (half + 1)] = cat(kblks).astype(BF16)
        for j in range(2):
            r0 = 256 * half + LANES * j
            qct_ref[0, r0:r0 + LANES, :] = (qblks[j] * (scale * LOG2E)).T.astype(BF16)
            vct_ref[0, r0:r0 + LANES, :] = v[:, j * LANES:(j + 1) * LANES].T.astype(BF16)


def _in_proj(x2, g, w, rc, rs1, rs2, batch, seq, tm):
    t, d = x2.shape
    n_cols = w.shape[1]
    tiles_per_seq = seq // tm
    row = lambda i: (i, 0)
    const = lambda i: (0, 0)
    ropei = lambda i: (i % tiles_per_seq, 0)
    tr = lambda i: (i // tiles_per_seq, 0, i % tiles_per_seq)
    nat = lambda n, dt: jax.ShapeDtypeStruct((t, n), dt)
    out_shape = [nat(256, BF16), nat(256, BF16), nat(256, BF16),
                 nat(256, BF16), nat(256, F32), nat(256, F32), nat(256, BF16), nat(256, F32),
                 jax.ShapeDtypeStruct((batch, C_WIDTH, seq), BF16),
                 nat(C_WIDTH, BF16),
                 jax.ShapeDtypeStruct((batch, C_WIDTH, seq), BF16)]
    out_specs = [pl.BlockSpec((tm, 256), row)] * 8 + [
        pl.BlockSpec((1, C_WIDTH, tm), tr), pl.BlockSpec((tm, C_WIDTH), row),
        pl.BlockSpec((1, C_WIDTH, tm), tr)]
    return pl.pallas_call(
        _in_proj_kernel,
        grid=(t // tm,),
        in_specs=[pl.BlockSpec((tm, d), row), pl.BlockSpec((1, d), const),
                  pl.BlockSpec((d, n_cols), const),
                  pl.BlockSpec((tm, LANES), ropei), pl.BlockSpec((tm, LANES), ropei),
                  pl.BlockSpec((tm, LANES), ropei)],
        out_specs=out_specs,
        out_shape=out_shape,
        compiler_params=_cparams(("parallel",)),
        name="in_proj",
    )(x2, g, w, rc, rs1, rs2)


def _banded_kernel(q_ref, k_ref, v_ref, o_ref, lse_ref, *, seq_len, q_block, sub, k_win):
    i = pl.program_id(1)
    lane = lax.broadcasted_iota(jnp.int32, (sub, A_WIDTH), 1)
    qi = lax.broadcasted_iota(jnp.int32, (sub, k_win), 0)
    ki = lax.broadcasted_iota(jnp.int32, (sub, k_win), 1)
    for sb in range(q_block // sub):
        q0 = i * q_block + sb * sub
        ks = pl.multiple_of(jnp.clip(q0 - BAND_HALF, 0, seq_len - k_win), BAND_HALF)
        kwin = k_ref[0, pl.ds(ks, k_win), :]
        vwin = v_ref[0, pl.ds(ks, k_win), :]
        qblk = q_ref[0, sb * sub:(sb + 1) * sub, :]
        band = jnp.abs((ks + ki) - (q0 + qi)) <= BAND_HALF
        o_acc = jnp.zeros((sub, A_WIDTH), F32)
        lse_acc = jnp.zeros((sub, A_WIDTH), F32)
        for h in range(N_HEADS):
            in_head = (lane >= h * HEAD_DIM) & (lane < (h + 1) * HEAD_DIM)
            qh = jnp.where(in_head, qblk, jnp.zeros_like(qblk))
            s = jnp.where(band, _dot_nt(qh, kwin), NEG_INF)
            m = jnp.max(s, axis=-1, keepdims=True)
            p = jnp.exp(s - m)
            den = jnp.sum(p, axis=-1, keepdims=True)
            o = _dot(p.astype(BF16), vwin) / den
            o_acc = jnp.where(in_head, o, o_acc)
            lse_acc = jnp.where(in_head, m + jnp.log(den), lse_acc)
        o_ref[0, sb * sub:(sb + 1) * sub, :] = o_acc
        lse_ref[0, sb * sub:(sb + 1) * sub, :] = lse_acc


def _banded(q, k, v):
    n_seq, seq_len, w = q.shape
    q_block = min(512, seq_len)
    sub = min(128, seq_len)
    k_win = min(sub + 2 * BAND_HALF, seq_len)
    kern = functools.partial(_banded_kernel, seq_len=seq_len, q_block=q_block, sub=sub, k_win=k_win)
    full = pl.BlockSpec((1, seq_len, w), lambda s, i: (s, 0, 0))
    blk = pl.BlockSpec((1, q_block, w), lambda s, i: (s, i, 0))
    return pl.pallas_call(
        kern,
        grid=(n_seq, seq_len // q_block),
        in_specs=[blk, full, full],
        out_specs=[blk, blk],
        out_shape=[jax.ShapeDtypeStruct((n_seq, seq_len, w), F32)] * 2,
        compiler_params=_cparams(("parallel", "arbitrary")),
        name="banded_attn",
    )(q, k, v)


def _to_sub(t, batch, seq, dil):
    w = t.shape[-1]
    return t.reshape(batch, seq // dil, dil, w).transpose(0, 2, 1, 3).reshape(batch * dil, seq // dil, w)


def _from_sub(t, batch, seq, dil):
    w = t.shape[-1]
    return t.reshape(batch, dil, seq // dil, w).transpose(0, 2, 1, 3).reshape(batch * seq, w)


def _split3(x):
    hi = x.astype(BF16)
    r1 = x - hi.astype(F32)
    mid = r1.astype(BF16)
    lo = (r1 - mid.astype(F32)).astype(BF16)
    return hi, mid, lo


def _hgrn_kernel(lbl_ref, qf_ref, zf_ref, vf_ref, qb_ref, zb_ref, vb_ref,
                 of_ref, ob_ref, sf_ref, sb_ref, *, depth, layer, n_chunks):
    c = HGRN_CHUNK

    @pl.when(pl.program_id(2) == 0)
    def _():
        sf_ref[...] = jnp.zeros_like(sf_ref)
        sb_ref[...] = jnp.zeros_like(sb_ref)

    def lower_bound(logits):
        e = jnp.exp(logits - jnp.max(logits, axis=0, keepdims=True))
        p = e / jnp.sum(e, axis=0, keepdims=True)
        lb = jnp.zeros((1, LANES), F32)
        for j in range(1, layer + 1):
            lb = lb + p[j:j + 1, :]
        return lb

    lb_f = lower_bound(lbl_ref[0:depth, :])
    lb_b = lower_bound(lbl_ref[depth:2 * depth, :])

    ti = lax.broadcasted_iota(jnp.int32, (c, c), 0)
    si = lax.broadcasted_iota(jnp.int32, (c, c), 1)
    tril = si <= ti
    triu = si >= ti
    ltri = jnp.where(tril, 1.0, 0.0).astype(BF16)
    lane = lax.broadcasted_iota(jnp.int32, (c, LANES), 1)
    head0 = lane < HEAD_DIM
    er = lax.broadcasted_iota(jnp.int32, (LANES, LANES), 0) < HEAD_DIM
    ec = lax.broadcasted_iota(jnp.int32, (LANES, LANES), 1) < HEAD_DIM
    same_head = er == ec

    def gates(z, lb):
        f = lb + (1.0 - lb) * jax.nn.sigmoid(z)
        return jnp.log(f), (1.0 - lb) * jax.nn.sigmoid(-z)

    def cumsum(logf):
        hi, mid, lo = _split3(logf)
        return _dot(ltri, hi) + _dot(ltri, mid) + _dot(ltri, lo)

    def intra(qd, kd, v, tri):
        kd = kd.astype(BF16)
        zero = jnp.zeros_like(qd)
        s0 = jnp.where(tri, _dot_nt(jnp.where(head0, qd, zero).astype(BF16), kd), 0.0)
        s1 = jnp.where(tri, _dot_nt(jnp.where(head0, zero, qd).astype(BF16), kd), 0.0)
        return jnp.where(head0, _dot(s0.astype(BF16), v), _dot(s1.astype(BF16), v))

    def step(q, z, v, lb, st_ref, forward):
        logf, kk = gates(z, lb)
        a = cumsum(logf)
        a_last = a[c - 1:c, :]
        if forward:
            e = a
            mid = a[c // 2 - 1:c // 2, :]
            q_in = q * jnp.exp(a)
            k_st = kk * jnp.exp(a_last - a)
            tri = tril
        else:
            e = a - logf
            mid = e[c // 2:c // 2 + 1, :]
            q_in = q * jnp.exp(a_last - e)
            k_st = kk * jnp.exp(e)
            tri = triu
        sgn = 1.0 if forward else -1.0
        qd = q * jnp.exp(jnp.minimum(sgn * (e - mid), EXP_CLAMP))
        kd = kk * jnp.exp(jnp.minimum(sgn * (mid - e), EXP_CLAMP))
        st = st_ref[...]
        o = _dot_nt(q_in.astype(BF16), st.astype(BF16)) + intra(qd, kd, v, tri)
        ut = _dot_tn(v, k_st.astype(BF16))
        st_ref[...] = st * jnp.exp(a_last) + jnp.where(same_head, ut, 0.0)
        return o

    for j in range(n_chunks):
        r = slice(j * c, (j + 1) * c)
        of_ref[r, :] = step(qf_ref[r, :].astype(F32), zf_ref[r, :], vf_ref[r, :], lb_f, sf_ref, True)
    for j in reversed(range(n_chunks)):
        r = slice(j * c, (j + 1) * c)
        ob_ref[r, :] = step(qb_ref[r, :].astype(F32), zb_ref[r, :], vb_ref[r, :], lb_b, sb_ref, False)


def _hgrn(lb_logits2, q, zf, zb, v, batch, seq, layer, rows):
    t = q.shape[0]
    depth = lb_logits2.shape[0] // 2
    n = seq // rows
    fwd = lambda b, p, i: (b * n + i, p)
    bwd = lambda b, p, i: (b * n + (n - 1 - i), p)
    kern = functools.partial(_hgrn_kernel, depth=depth, layer=layer, n_chunks=rows // HGRN_CHUNK)
    spec = lambda im: pl.BlockSpec((rows, LANES), im)
    return pl.pallas_call(
        kern,
        grid=(batch, B_WIDTH // LANES, n),
        in_specs=[pl.BlockSpec((2 * depth, LANES), lambda b, p, i: (0, p)),
                  spec(fwd), spec(fwd), spec(fwd), spec(bwd), spec(bwd), spec(bwd)],
        out_specs=[spec(fwd), spec(bwd)],
        out_shape=[jax.ShapeDtypeStruct((t, B_WIDTH), F32)] * 2,
        scratch_shapes=[pltpu.VMEM((LANES, LANES), F32), pltpu.VMEM((LANES, LANES), F32)],
        compiler_params=_cparams(("parallel", "parallel", "arbitrary")),
        name="hgrn2",
    )(lb_logits2, q, zf, v, q, zb, v)


def _diff_kernel(lam_ref, g_ref, qt_ref, k_ref, vt_ref, o_ref, s1_ref, s2_ref, *, seq, tq, kc, unroll,
                 lam_init):
    lp = lam_ref[...]
    lam = (jnp.exp(jnp.sum(lp[0:1, :] * lp[1:2, :], axis=1, keepdims=True))
           - jnp.exp(jnp.sum(lp[2:3, :] * lp[3:4, :], axis=1, keepdims=True)) + lam_init)
    qt = qt_ref[0]
    row = lax.broadcasted_iota(jnp.int32, qt.shape, 0)
    zero = jnp.zeros_like(qt)
    q1 = jnp.where(row < HEAD_DIM, qt, zero)
    q2 = jnp.where(row < HEAD_DIM, zero, qt)

    n = seq // kc
    sub = 8

    def scores(j, q, s_ref, mrun):
        k0 = pl.multiple_of(j * kc, kc)
        s = _dot(k_ref[pl.ds(k0, kc), :], q)
        s_ref[pl.ds(k0, kc), :] = s
        return jnp.maximum(mrun, jnp.max(s.reshape(kc // sub, sub, tq), axis=0))

    def weights(j, s_ref, m, lrun, acc):
        k0 = pl.multiple_of(j * kc, kc)
        p = jnp.exp2(s_ref[pl.ds(k0, kc), :] - m)
        lrun = lrun + jnp.sum(p.reshape(kc // sub, sub, tq), axis=0)
        return lrun, acc + _dot(vt_ref[0, :, pl.ds(k0, kc)], p.astype(BF16))

    part = lambda val: jnp.full((sub, tq), val, F32)
    mat = jnp.zeros((2 * HEAD_DIM, tq), F32)
    colmax = lambda mrun: jnp.max(mrun, axis=0, keepdims=True)

    loop = functools.partial(lax.fori_loop, 0, n, unroll=unroll)
    m1 = colmax(loop(lambda j, mr: scores(j, q1, s1_ref, mr), part(NEG_INF)))

    def both(j, carry):
        lr, acc, mr = carry
        lr, acc = weights(j, s1_ref, m1, lr, acc)
        return lr, acc, scores(j, q2, s2_ref, mr)

    l1, a1, mr2 = loop(both, (part(0.0), mat, part(NEG_INF)))
    m2 = colmax(mr2)
    l2, a2 = loop(lambda j, c: weights(j, s2_ref, m2, *c), (part(0.0), mat))
    l1 = jnp.sum(l1, axis=0, keepdims=True)
    l2 = jnp.sum(l2, axis=0, keepdims=True)
    o = a1 / l1 - lam * (a2 / l2)
    ms = jnp.mean(o * o, axis=0, keepdims=True)
    y = o * lax.rsqrt(ms + EPS) * g_ref[...] * (1.0 - lam_init)
    o_ref[...] = y.T.astype(o_ref.dtype)


def _diff_attn(lam_p, g_col, qct, kc_nat, vct, batch, seq, layer, tq, kc):
    t = kc_nat.shape[0]
    lam_init = 0.8 - 0.6 * math.exp(-0.3 * layer)
    nq = seq // tq
    kern = functools.partial(_diff_kernel, seq=seq, tq=tq, kc=kc, unroll=min(8, seq // kc), lam_init=lam_init)
    return pl.pallas_call(
        kern,
        grid=(batch, N_HEADS, nq),
        in_specs=[pl.BlockSpec(lam_p.shape, lambda b, h, i: (0, 0)),
                  pl.BlockSpec((2 * HEAD_DIM, 1), lambda b, h, i: (0, 0)),
                  pl.BlockSpec((1, 2 * HEAD_DIM, tq), lambda b, h, i: (b, h, i)),
                  pl.BlockSpec((seq, 2 * HEAD_DIM), lambda b, h, i: (b, h)),
                  pl.BlockSpec((1, 2 * HEAD_DIM, seq), lambda b, h, i: (b, h, 0))],
        out_specs=pl.BlockSpec((tq, 2 * HEAD_DIM), lambda b, h, i: (b * nq + i, h)),
        out_shape=jax.ShapeDtypeStruct((t, C_WIDTH), BF16),
        scratch_shapes=[pltpu.VMEM((seq, tq), F32), pltpu.VMEM((seq, tq), F32)],
        compiler_params=_cparams(("parallel", "parallel", "arbitrary")),
        name="diff_attn",
    )(lam_p, g_col, qct, kc_nat, vct)


def _out_proj_kernel(x_ref, o1_ref, l1_ref, o2_ref, l2_ref, o3_ref, l3_ref,
                     of_ref, ob_ref, gb_ref, oc_ref, hg_ref, w_ref, g_ref, h_ref):
    l1, l2, l3 = l1_ref[...], l2_ref[...], l3_ref[...]
    mx = jnp.maximum(jnp.maximum(l1, l2), l3)
    w1, w2, w3 = jnp.exp(l1 - mx), jnp.exp(l2 - mx), jnp.exp(l3 - mx)
    oa = (w1 * o1_ref[...] + w2 * o2_ref[...] + w3 * o3_ref[...]) / (w1 + w2 + w3)

    y = of_ref[...] + ob_ref[...]
    y2 = y * y
    lane = lax.broadcasted_iota(jnp.int32, y.shape, 1)
    ms = jnp.zeros_like(y)
    for h in range(N_HEADS):
        in_head = (lane >= h * HEAD_DIM) & (lane < (h + 1) * HEAD_DIM)
        ms_h = jnp.sum(jnp.where(in_head, y2, 0.0), axis=-1, keepdims=True) * (1.0 / HEAD_DIM)
        ms = jnp.where(in_head, ms_h, ms)
    gb = gb_ref[...]
    ob = y * lax.rsqrt(ms + EPS) * hg_ref[...] * (gb * jax.nn.sigmoid(gb))

    cat = jnp.concatenate([oa.astype(BF16), ob.astype(BF16), oc_ref[...]], axis=1)
    mix = _dot(cat, w_ref[...])
    h_ref[...] = x_ref[...] + _rms(mix, g_ref[...])


def _out_proj(x2, a_outs, of, ob, gb, oc, hg, w, g, tm):
    t, d = x2.shape
    row = lambda i: (i, 0)
    const = lambda i: (0, 0)
    r256 = pl.BlockSpec((tm, 256), row)
    return pl.pallas_call(
        _out_proj_kernel,
        grid=(t // tm,),
        in_specs=[pl.BlockSpec((tm, d), row)] + [r256] * 9 + [
            pl.BlockSpec((tm, C_WIDTH), row), pl.BlockSpec((1, B_WIDTH), const),
            pl.BlockSpec(w.shape, const), pl.BlockSpec((1, d), const)],
        out_specs=pl.BlockSpec((tm, d), row),
        out_shape=jax.ShapeDtypeStruct((t, d), F32),
        compiler_params=_cparams(("parallel",)),
        name="out_proj",
    )(x2, *a_outs, of, ob, gb, oc, hg, w, g)


def _ffn_kernel(hp_ref, h_ref, hn_ref, gpre_ref, wup_ref, cw_ref, cb_ref, wdn_ref, gpost_ref,
                o_ref, xe_ref, act_ref, *, tm, tiles_per_seq, d_ff, cn):
    i = pl.program_id(0)
    halo = BF16_ROWS
    first = (i % tiles_per_seq) == 0
    last = (i % tiles_per_seq) == tiles_per_seq - 1
    gpre = gpre_ref[...]
    h = h_ref[...]
    xe_ref[halo:halo + tm, :] = _rms(h, gpre).astype(BF16)
    xe_ref[0:halo, :] = jnp.where(first, 0.0, _rms(hp_ref[...], gpre)).astype(BF16)
    xe_ref[halo + tm:2 * halo + tm, :] = jnp.where(last, 0.0, _rms(hn_ref[...], gpre)).astype(BF16)
    xe = xe_ref[...]

    def conv(c0):
        u = _dot(xe, wup_ref[:, c0:c0 + cn])
        cw = cw_ref[:, c0:c0 + cn]
        return (u[halo - 1:halo - 1 + tm] * cw[0:1] + u[halo:halo + tm] * cw[1:2]
                + u[halo + 1:halo + 1 + tm] * cw[2:3] + cb_ref[:, c0:c0 + cn])

    for c in range(d_ff // cn):
        gate = conv(c * cn)
        val = conv(d_ff + c * cn)
        act_ref[:, c * cn:(c + 1) * cn] = (gate * jax.nn.sigmoid(gate) * val).astype(BF16)
    ff = _dot(act_ref[...], wdn_ref[...])
    o_ref[...] = h + _rms(ff, gpost_ref[...])


def _ffn(h2, gpre, wup, cw, cb, wdn, gpost, seq, tm, cn):
    t, d = h2.shape
    d_ff = wdn.shape[0]
    halo = BF16_ROWS
    tiles_per_seq = seq // tm
    hb = tm // halo
    const = lambda i: (0, 0)
    kern = functools.partial(_ffn_kernel, tm=tm, tiles_per_seq=tiles_per_seq, d_ff=d_ff, cn=cn)
    return pl.pallas_call(
        kern,
        grid=(t // tm,),
        in_specs=[pl.BlockSpec((halo, d), lambda i: (jnp.maximum(i * hb - 1, 0), 0)),
                  pl.BlockSpec((tm, d), lambda i: (i, 0)),
                  pl.BlockSpec((halo, d), lambda i: (jnp.minimum((i + 1) * hb, t // halo - 1), 0)),
                  pl.BlockSpec((1, d), const),
                  pl.BlockSpec(wup.shape, const), pl.BlockSpec(cw.shape, const),
                  pl.BlockSpec(cb.shape, const), pl.BlockSpec(wdn.shape, const),
                  pl.BlockSpec((1, d), const)],
        out_specs=pl.BlockSpec((tm, d), lambda i: (i, 0)),
        out_shape=jax.ShapeDtypeStruct((t, d), F32),
        scratch_shapes=[pltpu.VMEM((tm + 2 * halo, d), BF16), pltpu.VMEM((tm, d_ff), BF16)],
        compiler_params=_cparams(("parallel",)),
        name="conv_ffn",
    )(h2, h2, h2, gpre, wup, cw, cb, wdn, gpost)


def _rope_tables(seq):
    pos = jnp.arange(seq, dtype=F32)
    inv = ROPE_THETA ** (-jnp.arange(0, ROPE_DIM, 2, dtype=F32) / ROPE_DIM)
    ang = pos[:, None] * inv[None, :]
    cos, sin = jnp.cos(ang), jnp.sin(ang)
    rest = HEAD_DIM - ROPE_DIM
    one, zero = jnp.ones((seq, rest), F32), jnp.zeros((seq, rest), F32)
    zh = jnp.zeros((seq, ROPE_HALF), F32)
    rep = LANES // HEAD_DIM
    rc = jnp.tile(jnp.concatenate([cos, cos, one], axis=1), (1, rep))
    rs1 = jnp.tile(jnp.concatenate([-sin, zh, zero], axis=1), (1, rep))
    rs2 = jnp.tile(jnp.concatenate([zh, sin, zero], axis=1), (1, rep))
    return rc, rs1, rs2


def kernel(x, w_in, w_out, lb_logits, hgrn_norm, diff_lambda, diff_norm, w_up, conv_w, conv_b,
           w_down, norm_pre_mix, norm_post_mix, norm_pre_ffn, norm_post_ffn):
    batch, seq, d = x.shape
    depth = w_in.shape[0]
    tm = min(512, seq)
    rc, rs1, rs2 = _rope_tables(seq)
    lbl2 = lb_logits.astype(F32).reshape(2 * depth, B_WIDTH)
    x2 = x.reshape(batch * seq, d)
    for l in range(depth):
        (qa, ka, va, qb, zf, zb, ib, gb, qct, kc, vct) = _in_proj(
            x2, norm_pre_mix[l].reshape(1, d), w_in[l].astype(BF16), rc, rs1, rs2, batch, seq, tm)

        a_outs = []
        for dil in DILATIONS:
            o, lse = _banded(_to_sub(qa, batch, seq, dil), _to_sub(ka, batch, seq, dil),
                             _to_sub(va, batch, seq, dil))
            a_outs += [_from_sub(o, batch, seq, dil), _from_sub(lse, batch, seq, dil)]

        of, ob = _hgrn(lbl2, qb, zf, zb, ib, batch, seq, l, rows=min(512, seq))

        oc = _diff_attn(diff_lambda[l].astype(F32), diff_norm[l].astype(F32).reshape(2 * HEAD_DIM, 1),
                        qct, kc, vct, batch, seq, l, tq=min(256, seq), kc=min(512, seq))

        hg = jnp.tile(hgrn_norm[l].astype(F32), N_HEADS).reshape(1, B_WIDTH)
        h2 = _out_proj(x2, a_outs, of, ob, gb, oc, hg, w_out[l].astype(BF16),
                       norm_post_mix[l].reshape(1, d), tm)

        x2 = _ffn(h2, norm_pre_ffn[l].reshape(1, d), w_up[l].astype(BF16), conv_w[l],
                  conv_b[l].reshape(1, -1), w_down[l].astype(BF16), norm_post_ffn[l].reshape(1, d),
                  seq, tm, cn=256)
    return x2.reshape(batch, seq, d)
```

```python
import functools
import math

import jax
import jax.numpy as jnp
from jax import lax
from jax.experimental import pallas as pl
from jax.experimental.pallas import tpu as pltpu

F32 = jnp.float32
BF16 = jnp.bfloat16

HEAD_DIM = 64
N_HEADS = 4
A_WIDTH = N_HEADS * HEAD_DIM
B_WIDTH = N_HEADS * HEAD_DIM
C_WIDTH = N_HEADS * 2 * HEAD_DIM
ROPE_THETA = 500000.0
ROPE_DIM = HEAD_DIM // 4
ROPE_HALF = ROPE_DIM // 2
DILATIONS = (1, 4, 16)
BAND_HALF = 64
HGRN_CHUNK = 64
CONV_WIDTH = 3
EPS = 1e-6
NEG_INF = -1e30
LOG2E = math.log2(math.e)
EXP_CLAMP = 80.0

LANES = 128
BF16_ROWS = 16
VMEM_LIMIT = 56 * 1024 * 1024

OFF_QA, OFF_KA, OFF_VA = 0, 256, 512
OFF_QB, OFF_ZF, OFF_ZB, OFF_IB, OFF_GB = 768, 1024, 1280, 1536, 1792
OFF_QC, OFF_KC, OFF_VC = 2048, 2560, 3072


def _cparams(sem):
    return pltpu.CompilerParams(dimension_semantics=sem, vmem_limit_bytes=VMEM_LIMIT)


def _rms(x, g):
    ms = jnp.mean(x * x, axis=-1, keepdims=True)
    return x * lax.rsqrt(ms + EPS) * g


def _dot(a, b):
    return jnp.dot(a, b, preferred_element_type=F32)


def _dot_nt(a, b):
    return lax.dot_general(a, b, (((1,), (1,)), ((), ())), preferred_element_type=F32)


def _dot_tn(a, b):
    return lax.dot_general(a, b, (((0,), (0,)), ((), ())), preferred_element_type=F32)


def _in_proj_kernel(x_ref, g_ref, w_ref, rc_ref, rs1_ref, rs2_ref,
                    qa_ref, ka_ref, va_ref, qb_ref, zf_ref, zb_ref, ib_ref, gb_ref,
                    qct_ref, kc_ref, vct_ref):
    xn = _rms(x_ref[...], g_ref[...]).astype(BF16)
    rc, rs1, rs2 = rc_ref[...], rs1_ref[...], rs2_ref[...]

    def proj(c0, n):
        return _dot(xn, w_ref[:, c0:c0 + n])

    def rope(a):
        outs = []
        for j in range(a.shape[1] // LANES):
            blk = a[:, j * LANES:(j + 1) * LANES]
            outs.append(blk * rc + pltpu.roll(blk, LANES - ROPE_HALF, 1) * rs1
                        + pltpu.roll(blk, ROPE_HALF, 1) * rs2)
        return outs

    def cat(blks):
        return jnp.concatenate(blks, axis=1)

    scale = 1.0 / math.sqrt(HEAD_DIM)
    qa_ref[...] = (cat(rope(proj(OFF_QA, 256))) * scale).astype(BF16)
    ka_ref[...] = cat(rope(proj(OFF_KA, 256))).astype(BF16)
    va_ref[...] = proj(OFF_VA, 256).astype(BF16)
    qb_ref[...] = proj(OFF_QB, 256).astype(BF16)
    zf_ref[...] = proj(OFF_ZF, 256)
    zb_ref[...] = proj(OFF_ZB, 256)
    ib_ref[...] = proj(OFF_IB, 256).astype(BF16)
    gb_ref[...] = proj(OFF_GB, 256)
    for half in range(2):
        qblks = rope(proj(OFF_QC + 256 * half, 256))
        kblks = rope(proj(OFF_KC + 256 * half, 256))
        v = proj(OFF_VC + 256 * half, 256)
        kc_ref[:, 256 * half:256 * (half + 1)] = cat(kblks).astype(BF16)
        for j in range(2):
            r0 = 256 * half + LANES * j
            qct_ref[0, r0:r0 + LANES, :] = (qblks[j] * (scale * LOG2E)).T.astype(BF16)
            vct_ref[0, r0:r0 + LANES, :] = v[:, j * LANES:(j + 1) * LANES].T.astype(BF16)


def _in_proj(x2, g, w, rc, rs1, rs2, batch, seq, tm):
    t, d = x2.shape
    n_cols = w.shape[1]
    tiles_per_seq = seq // tm
    row = lambda i: (i, 0)
    const = lambda i: (0, 0)
    ropei = lambda i: (i % tiles_per_seq, 0)
    tr = lambda i: (i // tiles_per_seq, 0, i % tiles_per_seq)
    nat = lambda n, dt: jax.ShapeDtypeStruct((t, n), dt)
    out_shape = [nat(256, BF16), nat(256, BF16), nat(256, BF16),
                 nat(256, BF16), nat(256, F32), nat(256, F32), nat(256, BF16), nat(256, F32),
                 jax.ShapeDtypeStruct((batch, C_WIDTH, seq), BF16),
                 nat(C_WIDTH, BF16),
                 jax.ShapeDtypeStruct((batch, C_WIDTH, seq), BF16)]
    out_specs = [pl.BlockSpec((tm, 256), row)] * 8 + [
        pl.BlockSpec((1, C_WIDTH, tm), tr), pl.BlockSpec((tm, C_WIDTH), row),
        pl.BlockSpec((1, C_WIDTH, tm), tr)]
    return pl.pallas_call(
        _in_proj_kernel,
        grid=(t // tm,),
        in_specs=[pl.BlockSpec((tm, d), row), pl.BlockSpec((1, d), const),
                  pl.BlockSpec((d, n_cols), const),
                  pl.BlockSpec((tm, LANES), ropei), pl.BlockSpec((tm, LANES), ropei),
                  pl.BlockSpec((tm, LANES), ropei)],
        out_specs=out_specs,
        out_shape=out_shape,
        compiler_params=_cparams(("parallel",)),
        name="in_proj",
    )(x2, g, w, rc, rs1, rs2)


def _banded_kernel(q_ref, k_ref, v_ref, o_ref, lse_ref, *, seq_len, q_block, sub, k_win):
    i = pl.program_id(1)
    rows = N_HEADS * sub
    lane = lax.broadcasted_iota(jnp.int32, (rows, A_WIDTH), 1)
    row_head = lax.broadcasted_iota(jnp.int32, (rows, A_WIDTH), 0) // sub
    own_lanes = (lane // HEAD_DIM) == row_head
    qi = lax.broadcasted_iota(jnp.int32, (rows, k_win), 0) % sub
    ki = lax.broadcasted_iota(jnp.int32, (rows, k_win), 1)
    for sb in range(q_block // sub):
        q0 = i * q_block + sb * sub
        ks = pl.multiple_of(jnp.clip(q0 - BAND_HALF, 0, seq_len - k_win), BAND_HALF)
        kwin = k_ref[0, pl.ds(ks, k_win), :]
        vwin = v_ref[0, pl.ds(ks, k_win), :]
        qblk = q_ref[0, sb * sub:(sb + 1) * sub, :]
        q4 = jnp.concatenate([qblk] * N_HEADS, axis=0)
        q4 = jnp.where(own_lanes, q4, jnp.zeros_like(q4))
        band = jnp.abs((ks + ki) - (q0 + qi)) <= BAND_HALF
        s = jnp.where(band, _dot_nt(q4, kwin), NEG_INF)
        m = jnp.max(s, axis=-1, keepdims=True)
        p = jnp.exp(s - m)
        den = jnp.sum(p, axis=-1, keepdims=True)
        o4 = jnp.where(own_lanes, _dot(p.astype(BF16), vwin) / den, 0.0)
        l4 = jnp.where(own_lanes, m + jnp.log(den), 0.0)
        o_acc, lse_acc = o4[0:sub], l4[0:sub]
        for h in range(1, N_HEADS):
            o_acc = o_acc + o4[h * sub:(h + 1) * sub]
            lse_acc = lse_acc + l4[h * sub:(h + 1) * sub]
        o_ref[0, sb * sub:(sb + 1) * sub, :] = o_acc
        lse_ref[0, sb * sub:(sb + 1) * sub, :] = lse_acc


def _banded(q, k, v):
    n_seq, seq_len, w = q.shape
    q_block = min(512, seq_len)
    sub = min(128, seq_len)
    k_win = min(sub + 2 * BAND_HALF, seq_len)
    kern = functools.partial(_banded_kernel, seq_len=seq_len, q_block=q_block, sub=sub, k_win=k_win)
    full = pl.BlockSpec((1, seq_len, w), lambda s, i: (s, 0, 0))
    blk = pl.BlockSpec((1, q_block, w), lambda s, i: (s, i, 0))
    return pl.pallas_call(
        kern,
        grid=(n_seq, seq_len // q_block),
        in_specs=[blk, full, full],
        out_specs=[blk, blk],
        out_shape=[jax.ShapeDtypeStruct((n_seq, seq_len, w), F32)] * 2,
        compiler_params=_cparams(("parallel", "arbitrary")),
        name="banded_attn",
    )(q, k, v)


def _to_sub(t, batch, seq, dil):
    w = t.shape[-1]
    return t.reshape(batch, seq // dil, dil, w).transpose(0, 2, 1, 3).reshape(batch * dil, seq // dil, w)


def _from_sub(t, batch, seq, dil):
    w = t.shape[-1]
    return t.reshape(batch, dil, seq // dil, w).transpose(0, 2, 1, 3).reshape(batch * seq, w)


def _split3(x):
    hi = x.astype(BF16)
    r1 = x - hi.astype(F32)
    mid = r1.astype(BF16)
    lo = (r1 - mid.astype(F32)).astype(BF16)
    return hi, mid, lo


def _hgrn_kernel(lbl_ref, qf_ref, zf_ref, vf_ref, qb_ref, zb_ref, vb_ref,
                 of_ref, ob_ref, sf_ref, sb_ref, *, depth, layer, n_chunks):
    c = HGRN_CHUNK

    @pl.when(pl.program_id(2) == 0)
    def _():
        sf_ref[...] = jnp.zeros_like(sf_ref)
        sb_ref[...] = jnp.zeros_like(sb_ref)

    def lower_bound(logits):
        e = jnp.exp(logits - jnp.max(logits, axis=0, keepdims=True))
        p = e / jnp.sum(e, axis=0, keepdims=True)
        lb = jnp.zeros((1, LANES), F32)
        for j in range(1, layer + 1):
            lb = lb + p[j:j + 1, :]
        return lb

    lb_f = lower_bound(lbl_ref[0:depth, :])
    lb_b = lower_bound(lbl_ref[depth:2 * depth, :])

    ti = lax.broadcasted_iota(jnp.int32, (c, c), 0)
    si = lax.broadcasted_iota(jnp.int32, (c, c), 1)
    tril = si <= ti
    triu = si >= ti
    ltri = jnp.where(tril, 1.0, 0.0).astype(BF16)
    lane = lax.broadcasted_iota(jnp.int32, (c, LANES), 1)
    head0 = lane < HEAD_DIM
    er = lax.broadcasted_iota(jnp.int32, (LANES, LANES), 0) < HEAD_DIM
    ec = lax.broadcasted_iota(jnp.int32, (LANES, LANES), 1) < HEAD_DIM
    same_head = er == ec

    def gates(z, lb):
        f = lb + (1.0 - lb) * jax.nn.sigmoid(z)
        return jnp.log(f), (1.0 - lb) * jax.nn.sigmoid(-z)

    def cumsum(logf):
        hi, mid, lo = _split3(logf)
        return _dot(ltri, hi) + _dot(ltri, mid) + _dot(ltri, lo)

    def intra(qd, kd, v, tri):
        kd = kd.astype(BF16)
        zero = jnp.zeros_like(qd)
        s0 = jnp.where(tri, _dot_nt(jnp.where(head0, qd, zero).astype(BF16), kd), 0.0)
        s1 = jnp.where(tri, _dot_nt(jnp.where(head0, zero, qd).astype(BF16), kd), 0.0)
        return jnp.where(head0, _dot(s0.astype(BF16), v), _dot(s1.astype(BF16), v))

    def step(q, z, v, lb, st_ref, forward):
        logf, kk = gates(z, lb)
        a = cumsum(logf)
        a_last = a[c - 1:c, :]
        if forward:
            e = a
            mid = a[c // 2 - 1:c // 2, :]
            q_in = q * jnp.exp(a)
            k_st = kk * jnp.exp(a_last - a)
            tri = tril
        else:
            e = a - logf
            mid = e[c // 2:c // 2 + 1, :]
            q_in = q * jnp.exp(a_last - e)
            k_st = kk * jnp.exp(e)
            tri = triu
        sgn = 1.0 if forward else -1.0
        qd = q * jnp.exp(jnp.minimum(sgn * (e - mid), EXP_CLAMP))
        kd = kk * jnp.exp(jnp.minimum(sgn * (mid - e), EXP_CLAMP))
        st = st_ref[...]
        o = _dot_nt(q_in.astype(BF16), st.astype(BF16)) + intra(qd, kd, v, tri)
        ut = _dot_tn(v, k_st.astype(BF16))
        st_ref[...] = st * jnp.exp(a_last) + jnp.where(same_head, ut, 0.0)
        return o

    for j in range(n_chunks):
        r = slice(j * c, (j + 1) * c)
        of_ref[r, :] = step(qf_ref[r, :].astype(F32), zf_ref[r, :], vf_ref[r, :], lb_f, sf_ref, True)
    for j in reversed(range(n_chunks)):
        r = slice(j * c, (j + 1) * c)
        ob_ref[r, :] = step(qb_ref[r, :].astype(F32), zb_ref[r, :], vb_ref[r, :], lb_b, sb_ref, False)


def _hgrn(lb_logits2, q, zf, zb, v, batch, seq, layer, rows):
    t = q.shape[0]
    depth = lb_logits2.shape[0] // 2
    n = seq // rows
    fwd = lambda b, p, i: (b * n + i, p)
    bwd = lambda b, p, i: (b * n + (n - 1 - i), p)
    kern = functools.partial(_hgrn_kernel, depth=depth, layer=layer, n_chunks=rows // HGRN_CHUNK)
    spec = lambda im: pl.BlockSpec((rows, LANES), im)
    return pl.pallas_call(
        kern,
        grid=(batch, B_WIDTH // LANES, n),
        in_specs=[pl.BlockSpec((2 * depth, LANES), lambda b, p, i: (0, p)),
                  spec(fwd), spec(fwd), spec(fwd), spec(bwd), spec(bwd), spec(bwd)],
        out_specs=[spec(fwd), spec(bwd)],
        out_shape=[jax.ShapeDtypeStruct((t, B_WIDTH), F32)] * 2,
        scratch_shapes=[pltpu.VMEM((LANES, LANES), F32), pltpu.VMEM((LANES, LANES), F32)],
        compiler_params=_cparams(("parallel", "parallel", "arbitrary")),
        name="hgrn2",
    )(lb_logits2, q, zf, v, q, zb, v)


def _diff_kernel(lam_ref, g_ref, qc_ref, qn_ref, k_ref, vt_ref, o_ref, s1_ref, s2_ref, m_ref, *,
                 seq, tq, kc, unroll, lam_init):
    lp = lam_ref[...]
    lam = (jnp.exp(jnp.sum(lp[0:1, :] * lp[1:2, :], axis=1, keepdims=True))
           - jnp.exp(jnp.sum(lp[2:3, :] * lp[3:4, :], axis=1, keepdims=True)) + lam_init)
    n = seq // kc
    sub = 8

    def split(qt):
        row = lax.broadcasted_iota(jnp.int32, qt.shape, 0)
        zero = jnp.zeros_like(qt)
        return jnp.where(row < HEAD_DIM, qt, zero), jnp.where(row < HEAD_DIM, zero, qt)

    def scores(k0, q, s_ref, mrun):
        s = _dot(k_ref[pl.ds(k0, kc), :], q)
        s_ref[pl.ds(k0, kc), :] = s
        return jnp.maximum(mrun, jnp.max(s.reshape(kc // sub, sub, tq), axis=0))

    def weights(k0, s_ref, m, lrun, acc):
        p = jnp.exp2(s_ref[pl.ds(k0, kc), :] - m)
        lrun = lrun + jnp.sum(p.reshape(kc // sub, sub, tq), axis=0)
        return lrun, acc + _dot(vt_ref[0, :, pl.ds(k0, kc)], p.astype(BF16))

    part = lambda val: jnp.full((sub, tq), val, F32)
    mat = jnp.zeros((2 * HEAD_DIM, tq), F32)
    loop = functools.partial(lax.fori_loop, 0, n, unroll=unroll)

    @pl.when(pl.program_id(2) == 0)
    def _():
        q1, q2 = split(qc_ref[0])

        def first(j, carry):
            k0 = pl.multiple_of(j * kc, kc)
            return scores(k0, q1, s1_ref, carry[0]), scores(k0, q2, s2_ref, carry[1])

        mr1, mr2 = loop(first, (part(NEG_INF), part(NEG_INF)))
        m_ref[0:sub, :] = mr1
        m_ref[sub:2 * sub, :] = mr2

    m1 = jnp.max(m_ref[0:sub, :], axis=0, keepdims=True)
    m2 = jnp.max(m_ref[sub:2 * sub, :], axis=0, keepdims=True)
    q1n, q2n = split(qn_ref[0])

    def body(j, carry):
        l1, a1, l2, a2, mr1, mr2 = carry
        k0 = pl.multiple_of(j * kc, kc)
        l1, a1 = weights(k0, s1_ref, m1, l1, a1)
        mr1 = scores(k0, q1n, s1_ref, mr1)
        l2, a2 = weights(k0, s2_ref, m2, l2, a2)
        mr2 = scores(k0, q2n, s2_ref, mr2)
        return l1, a1, l2, a2, mr1, mr2

    l1, a1, l2, a2, mr1, mr2 = loop(body, (part(0.0), mat, part(0.0), mat, part(NEG_INF), part(NEG_INF)))
    m_ref[0:sub, :] = mr1
    m_ref[sub:2 * sub, :] = mr2
    l1 = jnp.sum(l1, axis=0, keepdims=True)
    l2 = jnp.sum(l2, axis=0, keepdims=True)
    o = a1 / l1 - lam * (a2 / l2)
    ms = jnp.mean(o * o, axis=0, keepdims=True)
    y = o * lax.rsqrt(ms + EPS) * g_ref[...] * (1.0 - lam_init)
    o_ref[...] = y.T.astype(o_ref.dtype)


def _diff_attn(lam_p, g_col, qct, kc_nat, vct, batch, seq, layer, tq, kc):
    t = kc_nat.shape[0]
    lam_init = 0.8 - 0.6 * math.exp(-0.3 * layer)
    nq = seq // tq
    kern = functools.partial(_diff_kernel, seq=seq, tq=tq, kc=kc, unroll=min(8, seq // kc), lam_init=lam_init)
    return pl.pallas_call(
        kern,
        grid=(batch, N_HEADS, nq),
        in_specs=[pl.BlockSpec(lam_p.shape, lambda b, h, i: (0, 0)),
                  pl.BlockSpec((2 * HEAD_DIM, 1), lambda b, h, i: (0, 0)),
                  pl.BlockSpec((1, 2 * HEAD_DIM, tq), lambda b, h, i: (b, h, i)),
                  pl.BlockSpec((1, 2 * HEAD_DIM, tq), lambda b, h, i: (b, h, jnp.minimum(i + 1, nq - 1))),
                  pl.BlockSpec((seq, 2 * HEAD_DIM), lambda b, h, i: (b, h)),
                  pl.BlockSpec((1, 2 * HEAD_DIM, seq), lambda b, h, i: (b, h, 0))],
        out_specs=pl.BlockSpec((tq, 2 * HEAD_DIM), lambda b, h, i: (b * nq + i, h)),
        out_shape=jax.ShapeDtypeStruct((t, C_WIDTH), BF16),
        scratch_shapes=[pltpu.VMEM((seq, tq), F32), pltpu.VMEM((seq, tq), F32), pltpu.VMEM((16, tq), F32)],
        compiler_params=_cparams(("parallel", "parallel", "arbitrary")),
        name="diff_attn",
    )(lam_p, g_col, qct, qct, kc_nat, vct)


def _out_proj_kernel(x_ref, o1_ref, l1_ref, o2_ref, l2_ref, o3_ref, l3_ref,
                     of_ref, ob_ref, gb_ref, oc_ref, hg_ref, w_ref, g_ref, h_ref):
    l1, l2, l3 = l1_ref[...], l2_ref[...], l3_ref[...]
    mx = jnp.maximum(jnp.maximum(l1, l2), l3)
    w1, w2, w3 = jnp.exp(l1 - mx), jnp.exp(l2 - mx), jnp.exp(l3 - mx)
    oa = (w1 * o1_ref[...] + w2 * o2_ref[...] + w3 * o3_ref[...]) / (w1 + w2 + w3)

    y = of_ref[...] + ob_ref[...]
    y2 = y * y
    lane = lax.broadcasted_iota(jnp.int32, y.shape, 1)
    ms = jnp.zeros_like(y)
    for h in range(N_HEADS):
        in_head = (lane >= h * HEAD_DIM) & (lane < (h + 1) * HEAD_DIM)
        ms_h = jnp.sum(jnp.where(in_head, y2, 0.0), axis=-1, keepdims=True) * (1.0 / HEAD_DIM)
        ms = jnp.where(in_head, ms_h, ms)
    gb = gb_ref[...]
    ob = y * lax.rsqrt(ms + EPS) * hg_ref[...] * (gb * jax.nn.sigmoid(gb))

    cat = jnp.concatenate([oa.astype(BF16), ob.astype(BF16), oc_ref[...]], axis=1)
    mix = _dot(cat, w_ref[...])
    h_ref[...] = x_ref[...] + _rms(mix, g_ref[...])


def _out_proj(x2, a_outs, of, ob, gb, oc, hg, w, g, tm):
    t, d = x2.shape
    row = lambda i: (i, 0)
    const = lambda i: (0, 0)
    r256 = pl.BlockSpec((tm, 256), row)
    return pl.pallas_call(
        _out_proj_kernel,
        grid=(t // tm,),
        in_specs=[pl.BlockSpec((tm, d), row)] + [r256] * 9 + [
            pl.BlockSpec((tm, C_WIDTH), row), pl.BlockSpec((1, B_WIDTH), const),
            pl.BlockSpec(w.shape, const), pl.BlockSpec((1, d), const)],
        out_specs=pl.BlockSpec((tm, d), row),
        out_shape=jax.ShapeDtypeStruct((t, d), F32),
        compiler_params=_cparams(("parallel",)),
        name="out_proj",
    )(x2, *a_outs, of, ob, gb, oc, hg, w, g)


def _ffn_kernel(hp_ref, h_ref, hn_ref, gpre_ref, wup_ref, cw_ref, cb_ref, wdn_ref, gpost_ref,
                o_ref, xe_ref, act_ref, *, tm, tiles_per_seq, d_ff, cn):
    i = pl.program_id(0)
    halo = BF16_ROWS
    first = (i % tiles_per_seq) == 0
    last = (i % tiles_per_seq) == tiles_per_seq - 1
    gpre = gpre_ref[...]
    h = h_ref[...]
    xe_ref[halo:halo + tm, :] = _rms(h, gpre).astype(BF16)
    xe_ref[0:halo, :] = jnp.where(first, 0.0, _rms(hp_ref[...], gpre)).astype(BF16)
    xe_ref[halo + tm:2 * halo + tm, :] = jnp.where(last, 0.0, _rms(hn_ref[...], gpre)).astype(BF16)
    xe = xe_ref[...]

    def conv(c0):
        u = _dot(xe, wup_ref[:, c0:c0 + cn])
        cw = cw_ref[:, c0:c0 + cn]
        return (u[halo - 1:halo - 1 + tm] * cw[0:1] + u[halo:halo + tm] * cw[1:2]
                + u[halo + 1:halo + 1 + tm] * cw[2:3] + cb_ref[:, c0:c0 + cn])

    for c in range(d_ff // cn):
        gate = conv(c * cn)
        val = conv(d_ff + c * cn)
        act_ref[:, c * cn:(c + 1) * cn] = (gate * jax.nn.sigmoid(gate) * val).astype(BF16)
    ff = _dot(act_ref[...], wdn_ref[...])
    o_ref[...] = h + _rms(ff, gpost_ref[...])


def _ffn(h2, gpre, wup, cw, cb, wdn, gpost, seq, tm, cn):
    t, d = h2.shape
    d_ff = wdn.shape[0]
    halo = BF16_ROWS
    tiles_per_seq = seq // tm
    hb = tm // halo
    const = lambda i: (0, 0)
    kern = functools.partial(_ffn_kernel, tm=tm, tiles_per_seq=tiles_per_seq, d_ff=d_ff, cn=cn)
    return pl.pallas_call(
        kern,
        grid=(t // tm,),
        in_specs=[pl.BlockSpec((halo, d), lambda i: (jnp.maximum(i * hb - 1, 0), 0)),
                  pl.BlockSpec((tm, d), lambda i: (i, 0)),
                  pl.BlockSpec((halo, d), lambda i: (jnp.minimum((i + 1) * hb, t // halo - 1), 0)),
                  pl.BlockSpec((1, d), const),
                  pl.BlockSpec(wup.shape, const), pl.BlockSpec(cw.shape, const),
                  pl.BlockSpec(cb.shape, const), pl.BlockSpec(wdn.shape, const),
                  pl.BlockSpec((1, d), const)],
        out_specs=pl.BlockSpec((tm, d), lambda i: (i, 0)),
        out_shape=jax.ShapeDtypeStruct((t, d), F32),
        scratch_shapes=[pltpu.VMEM((tm + 2 * halo, d), BF16), pltpu.VMEM((tm, d_ff), BF16)],
        compiler_params=_cparams(("parallel",)),
        name="conv_ffn",
    )(h2, h2, h2, gpre, wup, cw, cb, wdn, gpost)


def _rope_tables(seq):
    pos = jnp.arange(seq, dtype=F32)
    inv = ROPE_THETA ** (-jnp.arange(0, ROPE_DIM, 2, dtype=F32) / ROPE_DIM)
    ang = pos[:, None] * inv[None, :]
    cos, sin = jnp.cos(ang), jnp.sin(ang)
    rest = HEAD_DIM - ROPE_DIM
    one, zero = jnp.ones((seq, rest), F32), jnp.zeros((seq, rest), F32)
    zh = jnp.zeros((seq, ROPE_HALF), F32)
    rep = LANES // HEAD_DIM
    rc = jnp.tile(jnp.concatenate([cos, cos, one], axis=1), (1, rep))
    rs1 = jnp.tile(jnp.concatenate([-sin, zh, zero], axis=1), (1, rep))
    rs2 = jnp.tile(jnp.concatenate([zh, sin, zero], axis=1), (1, rep))
    return rc, rs1, rs2


def kernel(x, w_in, w_out, lb_logits, hgrn_norm, diff_lambda, diff_norm, w_up, conv_w, conv_b,
           w_down, norm_pre_mix, norm_post_mix, norm_pre_ffn, norm_post_ffn):
    batch, seq, d = x.shape
    depth = w_in.shape[0]
    tm = min(512, seq)
    rc, rs1, rs2 = _rope_tables(seq)
    lbl2 = lb_logits.astype(F32).reshape(2 * depth, B_WIDTH)
    x2 = x.reshape(batch * seq, d)
    for l in range(depth):
        (qa, ka, va, qb, zf, zb, ib, gb, qct, kc, vct) = _in_proj(
            x2, norm_pre_mix[l].reshape(1, d), w_in[l].astype(BF16), rc, rs1, rs2, batch, seq, tm)

        a_outs = []
        for dil in DILATIONS:
            o, lse = _banded(_to_sub(qa, batch, seq, dil), _to_sub(ka, batch, seq, dil),
                             _to_sub(va, batch, seq, dil))
            a_outs += [_from_sub(o, batch, seq, dil), _from_sub(lse, batch, seq, dil)]

        of, ob = _hgrn(lbl2, qb, zf, zb, ib, batch, seq, l, rows=min(512, seq))

        oc = _diff_attn(diff_lambda[l].astype(F32), diff_norm[l].astype(F32).reshape(2 * HEAD_DIM, 1),
                        qct, kc, vct, batch, seq, l, tq=min(256, seq), kc=min(512, seq))

        hg = jnp.tile(hgrn_norm[l].astype(F32), N_HEADS).reshape(1, B_WIDTH)
        h2 = _out_proj(x2, a_outs, of, ob, gb, oc, hg, w_out[l].astype(BF16),
                       norm_post_mix[l].reshape(1, d), tm)

        x2 = _ffn(h2, norm_pre_ffn[l].reshape(1, d), w_up[l].astype(BF16), conv_w[l],
                  conv_b[l].reshape(1, -1), w_down[l].astype(BF16), norm_post_ffn[l].reshape(1, d),
                  seq, tm, cn=256)
    return x2.reshape(batch, seq, d)
```

```python
import functools
import math

import jax
import jax.numpy as jnp
from jax import lax
from jax.experimental import pallas as pl
from jax.experimental.pallas import tpu as pltpu

F32 = jnp.float32
BF16 = jnp.bfloat16

HEAD_DIM = 64
N_HEADS = 4
A_WIDTH = N_HEADS * HEAD_DIM
B_WIDTH = N_HEADS * HEAD_DIM
C_WIDTH = N_HEADS * 2 * HEAD_DIM
ROPE_THETA = 500000.0
ROPE_DIM = HEAD_DIM // 4
ROPE_HALF = ROPE_DIM // 2
DILATIONS = (1, 4, 16)
BAND_HALF = 64
HGRN_CHUNK = 64
CONV_WIDTH = 3
EPS = 1e-6
NEG_INF = -1e30
LOG2E = math.log2(math.e)
EXP_CLAMP = 80.0

LANES = 128
BF16_ROWS = 16
VMEM_LIMIT = 56 * 1024 * 1024

OFF_QA, OFF_KA, OFF_VA = 0, 256, 512
OFF_QB, OFF_ZF, OFF_ZB, OFF_IB, OFF_GB = 768, 1024, 1280, 1536, 1792
OFF_QC, OFF_KC, OFF_VC = 2048, 2560, 3072


def _cparams(sem):
    return pltpu.CompilerParams(dimension_semantics=sem, vmem_limit_bytes=VMEM_LIMIT)


def _rms(x, g):
    ms = jnp.mean(x * x, axis=-1, keepdims=True)
    return x * lax.rsqrt(ms + EPS) * g


def _dot(a, b):
    return jnp.dot(a, b, preferred_element_type=F32)


def _dot_nt(a, b):
    return lax.dot_general(a, b, (((1,), (1,)), ((), ())), preferred_element_type=F32)


def _dot_tn(a, b):
    return lax.dot_general(a, b, (((0,), (0,)), ((), ())), preferred_element_type=F32)


def _in_proj_kernel(x_ref, g_ref, w_ref, rc_ref, rs1_ref, rs2_ref,
                    qa_ref, ka_ref, va_ref, qa4_ref, ka4_ref, va4_ref, qa16_ref, ka16_ref, va16_ref,
                    qb_ref, zf_ref, zb_ref, ib_ref, gb_ref, qct_ref, kc_ref, vct_ref,
                    sq_ref, sk_ref, sv_ref):
    tm = x_ref.shape[0]
    xn = _rms(x_ref[...], g_ref[...]).astype(BF16)

    def emit_a(val, stage_ref, nat_ref, sub_refs):
        nat_ref[...] = val.astype(BF16)
        halves = range(A_WIDTH // LANES)
        for h in halves:
            stage_ref[h] = val[:, h * LANES:(h + 1) * LANES]
        for dil, ref in zip(DILATIONS[1:], sub_refs):
            for r in range(dil):
                ref[0, r, :, :] = jnp.concatenate(
                    [stage_ref[h, pl.ds(r, tm // dil, stride=dil), :] for h in halves], axis=1).astype(BF16)

    rc, rs1, rs2 = rc_ref[...], rs1_ref[...], rs2_ref[...]

    def proj(c0, n):
        return _dot(xn, w_ref[:, c0:c0 + n])

    def rope(a):
        outs = []
        for j in range(a.shape[1] // LANES):
            blk = a[:, j * LANES:(j + 1) * LANES]
            outs.append(blk * rc + pltpu.roll(blk, LANES - ROPE_HALF, 1) * rs1
                        + pltpu.roll(blk, ROPE_HALF, 1) * rs2)
        return outs

    def cat(blks):
        return jnp.concatenate(blks, axis=1)

    scale = 1.0 / math.sqrt(HEAD_DIM)
    emit_a(cat(rope(proj(OFF_QA, 256))) * scale, sq_ref, qa_ref, (qa4_ref, qa16_ref))
    emit_a(cat(rope(proj(OFF_KA, 256))), sk_ref, ka_ref, (ka4_ref, ka16_ref))
    emit_a(proj(OFF_VA, 256), sv_ref, va_ref, (va4_ref, va16_ref))
    qb_ref[...] = proj(OFF_QB, 256).astype(BF16)
    zf_ref[...] = proj(OFF_ZF, 256)
    zb_ref[...] = proj(OFF_ZB, 256)
    ib_ref[...] = proj(OFF_IB, 256).astype(BF16)
    gb_ref[...] = proj(OFF_GB, 256)
    for half in range(2):
        qblks = rope(proj(OFF_QC + 256 * half, 256))
        kblks = rope(proj(OFF_KC + 256 * half, 256))
        v = proj(OFF_VC + 256 * half, 256)
        kc_ref[:, 256 * half:256 * (half + 1)] = cat(kblks).astype(BF16)
        for j in range(2):
            r0 = 256 * half + LANES * j
            qct_ref[0, r0:r0 + LANES, :] = (qblks[j] * (scale * LOG2E)).T.astype(BF16)
            vct_ref[0, r0:r0 + LANES, :] = v[:, j * LANES:(j + 1) * LANES].T.astype(BF16)


def _in_proj(x2, g, w, rc, rs1, rs2, batch, seq, tm):
    t, d = x2.shape
    n_cols = w.shape[1]
    tiles_per_seq = seq // tm
    row = lambda i: (i, 0)
    const = lambda i: (0, 0)
    ropei = lambda i: (i % tiles_per_seq, 0)
    tr = lambda i: (i // tiles_per_seq, 0, i % tiles_per_seq)
    nat = lambda n, dt: jax.ShapeDtypeStruct((t, n), dt)
    sub_shape = lambda dil: jax.ShapeDtypeStruct((batch, dil, seq // dil, A_WIDTH), BF16)
    sub_spec = lambda dil: pl.BlockSpec((1, dil, tm // dil, A_WIDTH),
                                        lambda i: (i // tiles_per_seq, 0, i % tiles_per_seq, 0))
    out_shape = ([nat(256, BF16)] * 3
                 + [sub_shape(DILATIONS[1])] * 3 + [sub_shape(DILATIONS[2])] * 3
                 + [nat(256, BF16), nat(256, F32), nat(256, F32), nat(256, BF16), nat(256, F32)]
                 + [jax.ShapeDtypeStruct((batch, C_WIDTH, seq), BF16),
                    nat(C_WIDTH, BF16),
                    jax.ShapeDtypeStruct((batch, C_WIDTH, seq), BF16)])
    out_specs = ([pl.BlockSpec((tm, 256), row)] * 3 + [sub_spec(DILATIONS[1])] * 3 + [sub_spec(DILATIONS[2])] * 3
                 + [pl.BlockSpec((tm, 256), row)] * 5
                 + [pl.BlockSpec((1, C_WIDTH, tm), tr), pl.BlockSpec((tm, C_WIDTH), row),
                    pl.BlockSpec((1, C_WIDTH, tm), tr)])
    return pl.pallas_call(
        _in_proj_kernel,
        grid=(t // tm,),
        in_specs=[pl.BlockSpec((tm, d), row), pl.BlockSpec((1, d), const),
                  pl.BlockSpec((d, n_cols), const),
                  pl.BlockSpec((tm, LANES), ropei), pl.BlockSpec((tm, LANES), ropei),
                  pl.BlockSpec((tm, LANES), ropei)],
        out_specs=out_specs,
        out_shape=out_shape,
        scratch_shapes=[pltpu.VMEM((A_WIDTH // LANES, tm, LANES), F32)] * 3,
        compiler_params=_cparams(("parallel",)),
        name="in_proj",
    )(x2, g, w, rc, rs1, rs2)


def _banded_kernel(q_ref, k_ref, v_ref, o_ref, lse_ref, *, seq_len, q_block, sub, k_win):
    i = pl.program_id(1)
    rows = N_HEADS * sub
    lane = lax.broadcasted_iota(jnp.int32, (rows, A_WIDTH), 1)
    row_head = lax.broadcasted_iota(jnp.int32, (rows, A_WIDTH), 0) // sub
    own_lanes = (lane // HEAD_DIM) == row_head
    qi = lax.broadcasted_iota(jnp.int32, (rows, k_win), 0) % sub
    ki = lax.broadcasted_iota(jnp.int32, (rows, k_win), 1)
    n_sub = q_block // sub
    q0s = [i * q_block + sb * sub for sb in range(n_sub)]
    kss = [pl.multiple_of(jnp.clip(q0 - BAND_HALF, 0, seq_len - k_win), BAND_HALF) for q0 in q0s]

    def score(sb):
        qblk = q_ref[0, sb * sub:(sb + 1) * sub, :]
        q4 = jnp.concatenate([qblk] * N_HEADS, axis=0)
        q4 = jnp.where(own_lanes, q4, jnp.zeros_like(q4))
        return _dot_nt(q4, k_ref[0, pl.ds(kss[sb], k_win), :])

    def softmax(sb, s):
        band = jnp.abs((kss[sb] + ki) - (q0s[sb] + qi)) <= BAND_HALF
        s = jnp.where(band, s, NEG_INF)
        m = jnp.max(s, axis=-1, keepdims=True)
        p = jnp.exp(s - m)
        den = jnp.sum(p, axis=-1, keepdims=True)
        return p.astype(BF16), den, m + jnp.log(den)

    ss = [score(sb) for sb in range(n_sub)]
    sm = [softmax(sb, s) for sb, s in enumerate(ss)]
    pv = [_dot(p, v_ref[0, pl.ds(kss[sb], k_win), :]) for sb, (p, _, _) in enumerate(sm)]
    for sb in range(n_sub):
        _, den, lse = sm[sb]
        o4 = jnp.where(own_lanes, pv[sb] / den, 0.0)
        l4 = jnp.where(own_lanes, lse, 0.0)
        o_acc, lse_acc = o4[0:sub], l4[0:sub]
        for h in range(1, N_HEADS):
            o_acc = o_acc + o4[h * sub:(h + 1) * sub]
            lse_acc = lse_acc + l4[h * sub:(h + 1) * sub]
        o_ref[0, sb * sub:(sb + 1) * sub, :] = o_acc
        lse_ref[0, sb * sub:(sb + 1) * sub, :] = lse_acc


def _banded(q, k, v):
    n_seq, seq_len, w = q.shape
    q_block = min(512, seq_len)
    sub = min(128, seq_len)
    k_win = min(sub + 2 * BAND_HALF, seq_len)
    kern = functools.partial(_banded_kernel, seq_len=seq_len, q_block=q_block, sub=sub, k_win=k_win)
    full = pl.BlockSpec((1, seq_len, w), lambda s, i: (s, 0, 0))
    blk = pl.BlockSpec((1, q_block, w), lambda s, i: (s, i, 0))
    return pl.pallas_call(
        kern,
        grid=(n_seq, seq_len // q_block),
        in_specs=[blk, full, full],
        out_specs=[blk, blk],
        out_shape=[jax.ShapeDtypeStruct((n_seq, seq_len, w), F32)] * 2,
        compiler_params=_cparams(("parallel", "arbitrary")),
        name="banded_attn",
    )(q, k, v)


def _split3(x):
    hi = x.astype(BF16)
    r1 = x - hi.astype(F32)
    mid = r1.astype(BF16)
    lo = (r1 - mid.astype(F32)).astype(BF16)
    return hi, mid, lo


def _hgrn_kernel(lbl_ref, lbd_ref, qf_ref, zf_ref, vf_ref, qb_ref, zb_ref, vb_ref,
                 of_ref, ob_ref, sf_ref, sb_ref, *, depth, layer, n_chunks):
    c = HGRN_CHUNK
    rows = n_chunks * c
    group = lbd_ref.shape[0]

    @pl.when(pl.program_id(2) == 0)
    def _():
        sf_ref[...] = jnp.zeros_like(sf_ref)
        sb_ref[...] = jnp.zeros_like(sb_ref)

    def lower_bound(logits):
        e = jnp.exp(logits - jnp.max(logits, axis=0, keepdims=True))
        p = e / jnp.sum(e, axis=0, keepdims=True)
        lb = jnp.zeros((1, LANES), F32)
        for j in range(1, layer + 1):
            lb = lb + p[j:j + 1, :]
        return lb

    lb_f = lower_bound(lbl_ref[0:depth, :])
    lb_b = lower_bound(lbl_ref[depth:2 * depth, :])

    ti = lax.broadcasted_iota(jnp.int32, (c, c), 0)
    si = lax.broadcasted_iota(jnp.int32, (c, c), 1)
    tril = si <= ti
    triu = si >= ti
    lane_c = lax.broadcasted_iota(jnp.int32, (c, LANES), 1) < HEAD_DIM
    lane_b = lax.broadcasted_iota(jnp.int32, (rows, LANES), 1) < HEAD_DIM
    er = lax.broadcasted_iota(jnp.int32, (LANES, LANES), 0) < HEAD_DIM
    ec = lax.broadcasted_iota(jnp.int32, (LANES, LANES), 1) < HEAD_DIM
    same_head = er == ec
    lbd = lbd_ref[...]

    def cumsum(logf):
        parts = jnp.concatenate(_split3(logf), axis=1)
        outs = []
        for g in range(rows // group):
            r = _dot(lbd, parts[g * group:(g + 1) * group, :])
            outs.append(r[:, 0:LANES] + r[:, LANES:2 * LANES] + r[:, 2 * LANES:3 * LANES])
        return jnp.concatenate(outs, axis=0)

    def chunk_rows(x, off):
        return jnp.concatenate(
            [jnp.broadcast_to(x[j * c + off:j * c + off + 1, :], (c, LANES)) for j in range(n_chunks)], axis=0)

    def direction(q_ref, z_ref, v_ref, lb, st_ref, o_ref, forward):
        q = q_ref[...].astype(F32)
        z = z_ref[...]
        v = v_ref[...]
        logf = jnp.log(lb + (1.0 - lb) * jax.nn.sigmoid(z))
        kk = (1.0 - lb) * jax.nn.sigmoid(-z)
        a = cumsum(logf)
        last = chunk_rows(a, c - 1)
        if forward:
            e = a
            mid = chunk_rows(a, c // 2 - 1)
            q_in = q * jnp.exp(a)
            k_st = kk * jnp.exp(last - a)
            tri = tril
        else:
            e = a - logf
            mid = chunk_rows(e, c // 2)
            q_in = q * jnp.exp(last - e)
            k_st = kk * jnp.exp(e)
            tri = triu
        sgn = 1.0 if forward else -1.0
        qd = q * jnp.exp(jnp.minimum(sgn * (e - mid), EXP_CLAMP))
        kd = (kk * jnp.exp(jnp.minimum(sgn * (mid - e), EXP_CLAMP))).astype(BF16)
        q0 = jnp.where(lane_b, qd, 0.0).astype(BF16)
        q1 = jnp.where(lane_b, 0.0, qd).astype(BF16)
        q_in = q_in.astype(BF16)
        k_st = k_st.astype(BF16)
        rs = [slice(j * c, (j + 1) * c) for j in range(n_chunks)]
        s0 = [_dot_nt(q0[r], kd[r]) for r in rs]
        s1 = [_dot_nt(q1[r], kd[r]) for r in rs]
        ut = [_dot_tn(v[r], k_st[r]) for r in rs]
        s0 = [jnp.where(tri, s, 0.0).astype(BF16) for s in s0]
        s1 = [jnp.where(tri, s, 0.0).astype(BF16) for s in s1]
        intra = [jnp.where(lane_c, _dot(x0, v[r]), _dot(x1, v[r])) for x0, x1, r in zip(s0, s1, rs)]
        st = st_ref[...]
        states = [None] * n_chunks
        for j in (range(n_chunks) if forward else reversed(range(n_chunks))):
            states[j] = st.astype(BF16)
            st = st * jnp.exp(a[(j + 1) * c - 1:(j + 1) * c, :]) + jnp.where(same_head, ut[j], 0.0)
        st_ref[...] = st
        for j, r in enumerate(rs):
            o_ref[r, :] = _dot_nt(q_in[r], states[j]) + intra[j]

    direction(qf_ref, zf_ref, vf_ref, lb_f, sf_ref, of_ref, True)
    direction(qb_ref, zb_ref, vb_ref, lb_b, sb_ref, ob_ref, False)


def _hgrn(lb_logits2, q, zf, zb, v, batch, seq, layer, rows):
    t = q.shape[0]
    depth = lb_logits2.shape[0] // 2
    n = seq // rows
    fwd = lambda b, p, i: (b * n + i, p)
    bwd = lambda b, p, i: (b * n + (n - 1 - i), p)
    kern = functools.partial(_hgrn_kernel, depth=depth, layer=layer, n_chunks=rows // HGRN_CHUNK)
    spec = lambda im: pl.BlockSpec((rows, LANES), im)
    group = min(256, rows)
    idx = jnp.arange(group)
    lbd = ((idx[None, :] <= idx[:, None])
           & (idx[None, :] // HGRN_CHUNK == idx[:, None] // HGRN_CHUNK)).astype(BF16)
    return pl.pallas_call(
        kern,
        grid=(batch, B_WIDTH // LANES, n),
        in_specs=[pl.BlockSpec((2 * depth, LANES), lambda b, p, i: (0, p)),
                  pl.BlockSpec((group, group), lambda b, p, i: (0, 0)),
                  spec(fwd), spec(fwd), spec(fwd), spec(bwd), spec(bwd), spec(bwd)],
        out_specs=[spec(fwd), spec(bwd)],
        out_shape=[jax.ShapeDtypeStruct((t, B_WIDTH), F32)] * 2,
        scratch_shapes=[pltpu.VMEM((LANES, LANES), F32), pltpu.VMEM((LANES, LANES), F32)],
        compiler_params=_cparams(("parallel", "parallel", "arbitrary")),
        name="hgrn2",
    )(lb_logits2, lbd, q, zf, v, q, zb, v)


def _diff_kernel(lam_ref, g_ref, qc_ref, qn_ref, k_ref, vt_ref, o_ref, s1_ref, s2_ref, m_ref, *,
                 seq, tq, kc, unroll, lam_init):
    lp = lam_ref[...]
    lam = (jnp.exp(jnp.sum(lp[0:1, :] * lp[1:2, :], axis=1, keepdims=True))
           - jnp.exp(jnp.sum(lp[2:3, :] * lp[3:4, :], axis=1, keepdims=True)) + lam_init)
    n = seq // kc
    sub = 8

    def split(qt):
        row = lax.broadcasted_iota(jnp.int32, qt.shape, 0)
        zero = jnp.zeros_like(qt)
        return jnp.where(row < HEAD_DIM, qt, zero), jnp.where(row < HEAD_DIM, zero, qt)

    def scores(k0, q, s_ref, mrun):
        s = _dot(k_ref[pl.ds(k0, kc), :], q)
        s_ref[pl.ds(k0, kc), :] = s
        return jnp.maximum(mrun, jnp.max(s.reshape(kc // sub, sub, tq), axis=0))

    def weights(k0, s_ref, m, lrun, acc):
        p = jnp.exp2(s_ref[pl.ds(k0, kc), :] - m)
        lrun = lrun + jnp.sum(p.reshape(kc // sub, sub, tq), axis=0)
        return lrun, acc + _dot(vt_ref[0, :, pl.ds(k0, kc)], p.astype(BF16))

    part = lambda val: jnp.full((sub, tq), val, F32)
    mat = jnp.zeros((2 * HEAD_DIM, tq), F32)
    loop = functools.partial(lax.fori_loop, 0, n, unroll=unroll)

    @pl.when(pl.program_id(2) == 0)
    def _():
        q1, q2 = split(qc_ref[0])

        def first(j, carry):
            k0 = pl.multiple_of(j * kc, kc)
            return scores(k0, q1, s1_ref, carry[0]), scores(k0, q2, s2_ref, carry[1])

        mr1, mr2 = loop(first, (part(NEG_INF), part(NEG_INF)))
        m_ref[0:sub, :] = mr1
        m_ref[sub:2 * sub, :] = mr2

    m1 = jnp.max(m_ref[0:sub, :], axis=0, keepdims=True)
    m2 = jnp.max(m_ref[sub:2 * sub, :], axis=0, keepdims=True)
    q1n, q2n = split(qn_ref[0])

    def body(j, carry):
        l1, a1, l2, a2, mr1, mr2 = carry
        k0 = pl.multiple_of(j * kc, kc)
        l1, a1 = weights(k0, s1_ref, m1, l1, a1)
        mr1 = scores(k0, q1n, s1_ref, mr1)
        l2, a2 = weights(k0, s2_ref, m2, l2, a2)
        mr2 = scores(k0, q2n, s2_ref, mr2)
        return l1, a1, l2, a2, mr1, mr2

    l1, a1, l2, a2, mr1, mr2 = loop(body, (part(0.0), mat, part(0.0), mat, part(NEG_INF), part(NEG_INF)))
    m_ref[0:sub, :] = mr1
    m_ref[sub:2 * sub, :] = mr2
    l1 = jnp.sum(l1, axis=0, keepdims=True)
    l2 = jnp.sum(l2, axis=0, keepdims=True)
    o = a1 / l1 - lam * (a2 / l2)
    ms = jnp.mean(o * o, axis=0, keepdims=True)
    y = o * lax.rsqrt(ms + EPS) * g_ref[...] * (1.0 - lam_init)
    o_ref[...] = y.T.astype(o_ref.dtype)


def _diff_attn(lam_p, g_col, qct, kc_nat, vct, batch, seq, layer, tq, kc):
    t = kc_nat.shape[0]
    lam_init = 0.8 - 0.6 * math.exp(-0.3 * layer)
    nq = seq // tq
    kern = functools.partial(_diff_kernel, seq=seq, tq=tq, kc=kc, unroll=min(8, seq // kc), lam_init=lam_init)
    return pl.pallas_call(
        kern,
        grid=(batch, N_HEADS, nq),
        in_specs=[pl.BlockSpec(lam_p.shape, lambda b, h, i: (0, 0)),
                  pl.BlockSpec((2 * HEAD_DIM, 1), lambda b, h, i: (0, 0)),
                  pl.BlockSpec((1, 2 * HEAD_DIM, tq), lambda b, h, i: (b, h, i)),
                  pl.BlockSpec((1, 2 * HEAD_DIM, tq), lambda b, h, i: (b, h, jnp.minimum(i + 1, nq - 1))),
                  pl.BlockSpec((seq, 2 * HEAD_DIM), lambda b, h, i: (b, h)),
                  pl.BlockSpec((1, 2 * HEAD_DIM, seq), lambda b, h, i: (b, h, 0))],
        out_specs=pl.BlockSpec((tq, 2 * HEAD_DIM), lambda b, h, i: (b * nq + i, h)),
        out_shape=jax.ShapeDtypeStruct((t, C_WIDTH), BF16),
        scratch_shapes=[pltpu.VMEM((seq, tq), F32), pltpu.VMEM((seq, tq), F32), pltpu.VMEM((16, tq), F32)],
        compiler_params=_cparams(("parallel", "parallel", "arbitrary")),
        name="diff_attn",
    )(lam_p, g_col, qct, qct, kc_nat, vct)


def _out_proj_kernel(x_ref, o1_ref, l1_ref, o2_ref, l2_ref, o3_ref, l3_ref,
                     of_ref, ob_ref, gb_ref, oc_ref, hg_ref, w_ref, g_ref, h_ref, *stage_refs):
    tm = x_ref.shape[0]

    def natural(ref, dil, stage_ref):
        halves = range(A_WIDTH // LANES)
        for r in range(dil):
            blk = ref[0, r]
            for h in halves:
                stage_ref[h, pl.ds(r, tm // dil, stride=dil), :] = blk[:, h * LANES:(h + 1) * LANES]
        return jnp.concatenate([stage_ref[h] for h in halves], axis=1)

    o2, l2 = natural(o2_ref, DILATIONS[1], stage_refs[0]), natural(l2_ref, DILATIONS[1], stage_refs[1])
    o3, l3 = natural(o3_ref, DILATIONS[2], stage_refs[2]), natural(l3_ref, DILATIONS[2], stage_refs[3])
    l1 = l1_ref[...]
    mx = jnp.maximum(jnp.maximum(l1, l2), l3)
    w1, w2, w3 = jnp.exp(l1 - mx), jnp.exp(l2 - mx), jnp.exp(l3 - mx)
    oa = (w1 * o1_ref[...] + w2 * o2 + w3 * o3) / (w1 + w2 + w3)

    y = of_ref[...] + ob_ref[...]
    y2 = y * y
    lane = lax.broadcasted_iota(jnp.int32, y.shape, 1)
    ms = jnp.zeros_like(y)
    for h in range(N_HEADS):
        in_head = (lane >= h * HEAD_DIM) & (lane < (h + 1) * HEAD_DIM)
        ms_h = jnp.sum(jnp.where(in_head, y2, 0.0), axis=-1, keepdims=True) * (1.0 / HEAD_DIM)
        ms = jnp.where(in_head, ms_h, ms)
    gb = gb_ref[...]
    ob = y * lax.rsqrt(ms + EPS) * hg_ref[...] * (gb * jax.nn.sigmoid(gb))

    cat = jnp.concatenate([oa.astype(BF16), ob.astype(BF16), oc_ref[...]], axis=1)
    mix = _dot(cat, w_ref[...])
    h_ref[...] = x_ref[...] + _rms(mix, g_ref[...])


def _out_proj(x2, a_outs, of, ob, gb, oc, hg, w, g, seq, tm):
    t, d = x2.shape
    tiles_per_seq = seq // tm
    row = lambda i: (i, 0)
    const = lambda i: (0, 0)
    r256 = pl.BlockSpec((tm, 256), row)
    sub = lambda dil: pl.BlockSpec((1, dil, tm // dil, A_WIDTH),
                                   lambda i: (i // tiles_per_seq, 0, i % tiles_per_seq, 0))
    a_specs = [r256, r256] + [sub(DILATIONS[1])] * 2 + [sub(DILATIONS[2])] * 2
    return pl.pallas_call(
        _out_proj_kernel,
        grid=(t // tm,),
        in_specs=[pl.BlockSpec((tm, d), row)] + a_specs + [r256] * 3 + [
            pl.BlockSpec((tm, C_WIDTH), row), pl.BlockSpec((1, B_WIDTH), const),
            pl.BlockSpec(w.shape, const), pl.BlockSpec((1, d), const)],
        out_specs=pl.BlockSpec((tm, d), row),
        out_shape=jax.ShapeDtypeStruct((t, d), F32),
        scratch_shapes=[pltpu.VMEM((A_WIDTH // LANES, tm, LANES), F32)] * 4,
        compiler_params=_cparams(("parallel",)),
        name="out_proj",
    )(x2, *a_outs, of, ob, gb, oc, hg, w, g)


def _ffn_kernel(hp_ref, h_ref, hn_ref, gpre_ref, wup_ref, cw_ref, cb_ref, wdn_ref, gpost_ref,
                o_ref, xe_ref, act_ref, *, tm, tiles_per_seq, d_ff, cn):
    i = pl.program_id(0)
    halo = BF16_ROWS
    first = (i % tiles_per_seq) == 0
    last = (i % tiles_per_seq) == tiles_per_seq - 1
    gpre = gpre_ref[...]
    h = h_ref[...]
    xe_ref[halo:halo + tm, :] = _rms(h, gpre).astype(BF16)
    xe_ref[0:halo, :] = jnp.where(first, 0.0, _rms(hp_ref[...], gpre)).astype(BF16)
    xe_ref[halo + tm:2 * halo + tm, :] = jnp.where(last, 0.0, _rms(hn_ref[...], gpre)).astype(BF16)
    xe = xe_ref[...]

    def conv(c0):
        u = _dot(xe, wup_ref[:, c0:c0 + cn])
        cw = cw_ref[:, c0:c0 + cn]
        return (u[halo - 1:halo - 1 + tm] * cw[0:1] + u[halo:halo + tm] * cw[1:2]
                + u[halo + 1:halo + 1 + tm] * cw[2:3] + cb_ref[:, c0:c0 + cn])

    for c in range(d_ff // cn):
        gate = conv(c * cn)
        val = conv(d_ff + c * cn)
        act_ref[:, c * cn:(c + 1) * cn] = (gate * jax.nn.sigmoid(gate) * val).astype(BF16)
    ff = _dot(act_ref[...], wdn_ref[...])
    o_ref[...] = h + _rms(ff, gpost_ref[...])


def _ffn(h2, gpre, wup, cw, cb, wdn, gpost, seq, tm, cn):
    t, d = h2.shape
    d_ff = wdn.shape[0]
    halo = BF16_ROWS
    tiles_per_seq = seq // tm
    hb = tm // halo
    const = lambda i: (0, 0)
    kern = functools.partial(_ffn_kernel, tm=tm, tiles_per_seq=tiles_per_seq, d_ff=d_ff, cn=cn)
    return pl.pallas_call(
        kern,
        grid=(t // tm,),
        in_specs=[pl.BlockSpec((halo, d), lambda i: (jnp.maximum(i * hb - 1, 0), 0)),
                  pl.BlockSpec((tm, d), lambda i: (i, 0)),
                  pl.BlockSpec((halo, d), lambda i: (jnp.minimum((i + 1) * hb, t // halo - 1), 0)),
                  pl.BlockSpec((1, d), const),
                  pl.BlockSpec(wup.shape, const), pl.BlockSpec(cw.shape, const),
                  pl.BlockSpec(cb.shape, const), pl.BlockSpec(wdn.shape, const),
                  pl.BlockSpec((1, d), const)],
        out_specs=pl.BlockSpec((tm, d), lambda i: (i, 0)),
        out_shape=jax.ShapeDtypeStruct((t, d), F32),
        scratch_shapes=[pltpu.VMEM((tm + 2 * halo, d), BF16), pltpu.VMEM((tm, d_ff), BF16)],
        compiler_params=_cparams(("parallel",)),
        name="conv_ffn",
    )(h2, h2, h2, gpre, wup, cw, cb, wdn, gpost)


def _rope_tables(seq):
    pos = jnp.arange(seq, dtype=F32)
    inv = ROPE_THETA ** (-jnp.arange(0, ROPE_DIM, 2, dtype=F32) / ROPE_DIM)
    ang = pos[:, None] * inv[None, :]
    cos, sin = jnp.cos(ang), jnp.sin(ang)
    rest = HEAD_DIM - ROPE_DIM
    one, zero = jnp.ones((seq, rest), F32), jnp.zeros((seq, rest), F32)
    zh = jnp.zeros((seq, ROPE_HALF), F32)
    rep = LANES // HEAD_DIM
    rc = jnp.tile(jnp.concatenate([cos, cos, one], axis=1), (1, rep))
    rs1 = jnp.tile(jnp.concatenate([-sin, zh, zero], axis=1), (1, rep))
    rs2 = jnp.tile(jnp.concatenate([zh, sin, zero], axis=1), (1, rep))
    return rc, rs1, rs2


def kernel(x, w_in, w_out, lb_logits, hgrn_norm, diff_lambda, diff_norm, w_up, conv_w, conv_b,
           w_down, norm_pre_mix, norm_post_mix, norm_pre_ffn, norm_post_ffn):
    batch, seq, d = x.shape
    depth = w_in.shape[0]
    tm = min(512, seq)
    rc, rs1, rs2 = _rope_tables(seq)
    lbl2 = lb_logits.astype(F32).reshape(2 * depth, B_WIDTH)
    x2 = x.reshape(batch * seq, d)
    for l in range(depth):
        (qa, ka, va, qa4, ka4, va4, qa16, ka16, va16, qb, zf, zb, ib, gb, qct, kc, vct) = _in_proj(
            x2, norm_pre_mix[l].reshape(1, d), w_in[l].astype(BF16), rc, rs1, rs2, batch, seq, tm)

        a_outs = []
        for dil, qkv in zip(DILATIONS, ((qa, ka, va), (qa4, ka4, va4), (qa16, ka16, va16))):
            o, lse = _banded(*(a.reshape(batch * dil, seq // dil, A_WIDTH) for a in qkv))
            shape = (batch * seq, A_WIDTH) if dil == 1 else (batch, dil, seq // dil, A_WIDTH)
            a_outs += [o.reshape(shape), lse.reshape(shape)]

        of, ob = _hgrn(lbl2, qb, zf, zb, ib, batch, seq, l, rows=min(512, seq))

        oc = _diff_attn(diff_lambda[l].astype(F32), diff_norm[l].astype(F32).reshape(2 * HEAD_DIM, 1),
                        qct, kc, vct, batch, seq, l, tq=min(256, seq), kc=min(512, seq))

        hg = jnp.tile(hgrn_norm[l].astype(F32), N_HEADS).reshape(1, B_WIDTH)
        h2 = _out_proj(x2, a_outs, of, ob, gb, oc, hg, w_out[l].astype(BF16),
                       norm_post_mix[l].reshape(1, d), seq, tm)

        x2 = _ffn(h2, norm_pre_ffn[l].reshape(1, d), w_up[l].astype(BF16), conv_w[l],
                  conv_b[l].reshape(1, -1), w_down[l].astype(BF16), norm_post_ffn[l].reshape(1, d),
                  seq, tm, cn=256)
    return x2.reshape(batch, seq, d)
```

```python
import functools
import math

import jax
import jax.numpy as jnp
from jax import lax
from jax.experimental import pallas as pl
from jax.experimental.pallas import tpu as pltpu

F32 = jnp.float32
BF16 = jnp.bfloat16

HEAD_DIM = 64
N_HEADS = 4
A_WIDTH = N_HEADS * HEAD_DIM
B_WIDTH = N_HEADS * HEAD_DIM
C_WIDTH = N_HEADS * 2 * HEAD_DIM
ROPE_THETA = 500000.0
ROPE_DIM = HEAD_DIM // 4
ROPE_HALF = ROPE_DIM // 2
DILATIONS = (1, 4, 16)
BAND_HALF = 64
HGRN_CHUNK = 64
CONV_WIDTH = 3
EPS = 1e-6
NEG_INF = -1e30
LOG2E = math.log2(math.e)
EXP_CLAMP = 80.0

LANES = 128
BF16_ROWS = 16
VMEM_LIMIT = 56 * 1024 * 1024

OFF_QA, OFF_KA, OFF_VA = 0, 256, 512
OFF_QB, OFF_ZF, OFF_ZB, OFF_IB, OFF_GB = 768, 1024, 1280, 1536, 1792
OFF_QC, OFF_KC, OFF_VC = 2048, 2560, 3072


def _cparams(sem, flags=None):
    return pltpu.CompilerParams(dimension_semantics=sem, vmem_limit_bytes=VMEM_LIMIT, flags=flags)


def _rms(x, g):
    ms = jnp.mean(x * x, axis=-1, keepdims=True)
    return x * lax.rsqrt(ms + EPS) * g


def _dot(a, b):
    return jnp.dot(a, b, preferred_element_type=F32)


def _dot_nt(a, b):
    return lax.dot_general(a, b, (((1,), (1,)), ((), ())), preferred_element_type=F32)


def _dot_tn(a, b):
    return lax.dot_general(a, b, (((0,), (0,)), ((), ())), preferred_element_type=F32)


def _in_proj_kernel(x_ref, g_ref, w_ref, rc_ref, rs1_ref, rs2_ref,
                    qa_ref, ka_ref, va_ref, qa4_ref, ka4_ref, va4_ref, qa16_ref, ka16_ref, va16_ref,
                    qb_ref, zf_ref, zb_ref, ib_ref, gb_ref, qct_ref, kc_ref, vct_ref,
                    sq_ref, sk_ref, sv_ref):
    tm = x_ref.shape[0]
    xn = _rms(x_ref[...], g_ref[...]).astype(BF16)

    def emit_a(val, stage_ref, nat_ref, sub_refs):
        nat_ref[...] = val.astype(BF16)
        halves = range(A_WIDTH // LANES)
        for h in halves:
            stage_ref[h] = val[:, h * LANES:(h + 1) * LANES]
        for dil, ref in zip(DILATIONS[1:], sub_refs):
            for r in range(dil):
                ref[0, r, :, :] = jnp.concatenate(
                    [stage_ref[h, pl.ds(r, tm // dil, stride=dil), :] for h in halves], axis=1).astype(BF16)

    rc, rs1, rs2 = rc_ref[...], rs1_ref[...], rs2_ref[...]

    def proj(c0, n):
        return _dot(xn, w_ref[:, c0:c0 + n])

    def rope(a):
        outs = []
        for j in range(a.shape[1] // LANES):
            blk = a[:, j * LANES:(j + 1) * LANES]
            outs.append(blk * rc + pltpu.roll(blk, LANES - ROPE_HALF, 1) * rs1
                        + pltpu.roll(blk, ROPE_HALF, 1) * rs2)
        return outs

    def cat(blks):
        return jnp.concatenate(blks, axis=1)

    scale = 1.0 / math.sqrt(HEAD_DIM)
    emit_a(cat(rope(proj(OFF_QA, 256))) * scale, sq_ref, qa_ref, (qa4_ref, qa16_ref))
    emit_a(cat(rope(proj(OFF_KA, 256))), sk_ref, ka_ref, (ka4_ref, ka16_ref))
    emit_a(proj(OFF_VA, 256), sv_ref, va_ref, (va4_ref, va16_ref))
    qb_ref[...] = proj(OFF_QB, 256).astype(BF16)
    zf_ref[...] = proj(OFF_ZF, 256)
    zb_ref[...] = proj(OFF_ZB, 256)
    ib_ref[...] = proj(OFF_IB, 256).astype(BF16)
    gb_ref[...] = proj(OFF_GB, 256)
    for half in range(2):
        qblks = rope(proj(OFF_QC + 256 * half, 256))
        kblks = rope(proj(OFF_KC + 256 * half, 256))
        v = proj(OFF_VC + 256 * half, 256)
        kc_ref[:, 256 * half:256 * (half + 1)] = cat(kblks).astype(BF16)
        for j in range(2):
            r0 = 256 * half + LANES * j
            qct_ref[0, r0:r0 + LANES, :] = (qblks[j] * (scale * LOG2E)).T.astype(BF16)
            vct_ref[0, r0:r0 + LANES, :] = v[:, j * LANES:(j + 1) * LANES].T.astype(BF16)


def _in_proj(x2, g, w, rc, rs1, rs2, batch, seq, tm):
    t, d = x2.shape
    n_cols = w.shape[1]
    tiles_per_seq = seq // tm
    row = lambda i: (i, 0)
    const = lambda i: (0, 0)
    ropei = lambda i: (i % tiles_per_seq, 0)
    tr = lambda i: (i // tiles_per_seq, 0, i % tiles_per_seq)
    nat = lambda n, dt: jax.ShapeDtypeStruct((t, n), dt)
    sub_shape = lambda dil: jax.ShapeDtypeStruct((batch, dil, seq // dil, A_WIDTH), BF16)
    sub_spec = lambda dil: pl.BlockSpec((1, dil, tm // dil, A_WIDTH),
                                        lambda i: (i // tiles_per_seq, 0, i % tiles_per_seq, 0))
    out_shape = ([nat(256, BF16)] * 3
                 + [sub_shape(DILATIONS[1])] * 3 + [sub_shape(DILATIONS[2])] * 3
                 + [nat(256, BF16), nat(256, F32), nat(256, F32), nat(256, BF16), nat(256, F32)]
                 + [jax.ShapeDtypeStruct((batch, C_WIDTH, seq), BF16),
                    nat(C_WIDTH, BF16),
                    jax.ShapeDtypeStruct((batch, C_WIDTH, seq), BF16)])
    out_specs = ([pl.BlockSpec((tm, 256), row)] * 3 + [sub_spec(DILATIONS[1])] * 3 + [sub_spec(DILATIONS[2])] * 3
                 + [pl.BlockSpec((tm, 256), row)] * 5
                 + [pl.BlockSpec((1, C_WIDTH, tm), tr), pl.BlockSpec((tm, C_WIDTH), row),
                    pl.BlockSpec((1, C_WIDTH, tm), tr)])
    return pl.pallas_call(
        _in_proj_kernel,
        grid=(t // tm,),
        in_specs=[pl.BlockSpec((tm, d), row), pl.BlockSpec((1, d), const),
                  pl.BlockSpec((d, n_cols), const),
                  pl.BlockSpec((tm, LANES), ropei), pl.BlockSpec((tm, LANES), ropei),
                  pl.BlockSpec((tm, LANES), ropei)],
        out_specs=out_specs,
        out_shape=out_shape,
        scratch_shapes=[pltpu.VMEM((A_WIDTH // LANES, tm, LANES), F32)] * 3,
        compiler_params=_cparams(("parallel",)),
        name="in_proj",
    )(x2, g, w, rc, rs1, rs2)


def _banded_kernel(q_ref, k_ref, v_ref, o_ref, lse_ref, *, seq_len, q_block, sub, k_win):
    i = pl.program_id(1)
    rows = N_HEADS * sub
    lane = lax.broadcasted_iota(jnp.int32, (rows, A_WIDTH), 1)
    row_head = lax.broadcasted_iota(jnp.int32, (rows, A_WIDTH), 0) // sub
    own_lanes = (lane // HEAD_DIM) == row_head
    qi = lax.broadcasted_iota(jnp.int32, (rows, k_win), 0) % sub
    ki = lax.broadcasted_iota(jnp.int32, (rows, k_win), 1)
    n_sub = q_block // sub
    q0s = [i * q_block + sb * sub for sb in range(n_sub)]
    kss = [pl.multiple_of(jnp.clip(q0 - BAND_HALF, 0, seq_len - k_win), BAND_HALF) for q0 in q0s]

    def score(sb):
        qblk = q_ref[0, sb * sub:(sb + 1) * sub, :]
        q4 = jnp.concatenate([qblk] * N_HEADS, axis=0)
        q4 = jnp.where(own_lanes, q4, jnp.zeros_like(q4))
        return _dot_nt(q4, k_ref[0, pl.ds(kss[sb], k_win), :])

    def softmax(sb, s):
        band = jnp.abs((kss[sb] + ki) - (q0s[sb] + qi)) <= BAND_HALF
        s = jnp.where(band, s, NEG_INF)
        m = jnp.max(s, axis=-1, keepdims=True)
        p = jnp.exp(s - m)
        den = jnp.sum(p, axis=-1, keepdims=True)
        return p.astype(BF16), den, m + jnp.log(den)

    ss = [score(sb) for sb in range(n_sub)]
    sm = [softmax(sb, s) for sb, s in enumerate(ss)]
    pv = [_dot(p, v_ref[0, pl.ds(kss[sb], k_win), :]) for sb, (p, _, _) in enumerate(sm)]
    for sb in range(n_sub):
        _, den, lse = sm[sb]
        o4 = jnp.where(own_lanes, pv[sb] / den, 0.0)
        l4 = jnp.where(own_lanes, lse, 0.0)
        o_acc, lse_acc = o4[0:sub], l4[0:sub]
        for h in range(1, N_HEADS):
            o_acc = o_acc + o4[h * sub:(h + 1) * sub]
            lse_acc = lse_acc + l4[h * sub:(h + 1) * sub]
        o_ref[0, sb * sub:(sb + 1) * sub, :] = o_acc
        lse_ref[0, sb * sub:(sb + 1) * sub, :] = lse_acc


def _banded(q, k, v):
    n_seq, seq_len, w = q.shape
    q_block = min(512, seq_len)
    sub = min(128, seq_len)
    k_win = min(sub + 2 * BAND_HALF, seq_len)
    kern = functools.partial(_banded_kernel, seq_len=seq_len, q_block=q_block, sub=sub, k_win=k_win)
    full = pl.BlockSpec((1, seq_len, w), lambda s, i: (s, 0, 0))
    blk = pl.BlockSpec((1, q_block, w), lambda s, i: (s, i, 0))
    return pl.pallas_call(
        kern,
        grid=(n_seq, seq_len // q_block),
        in_specs=[blk, full, full],
        out_specs=[blk, blk],
        out_shape=[jax.ShapeDtypeStruct((n_seq, seq_len, w), F32)] * 2,
        compiler_params=_cparams(("parallel", "arbitrary")),
        name="banded_attn",
    )(q, k, v)


def _split3(x):
    hi = x.astype(BF16)
    r1 = x - hi.astype(F32)
    mid = r1.astype(BF16)
    lo = (r1 - mid.astype(F32)).astype(BF16)
    return hi, mid, lo


def _hgrn_kernel(lbl_ref, lbd_ref, qf_ref, zf_ref, vf_ref, qb_ref, zb_ref, vb_ref,
                 of_ref, ob_ref, sf_ref, sb_ref, *, depth, layer, n_chunks):
    c = HGRN_CHUNK
    rows = n_chunks * c
    group = lbd_ref.shape[0]

    @pl.when(pl.program_id(2) == 0)
    def _():
        sf_ref[...] = jnp.zeros_like(sf_ref)
        sb_ref[...] = jnp.zeros_like(sb_ref)

    def lower_bound(logits):
        e = jnp.exp(logits - jnp.max(logits, axis=0, keepdims=True))
        p = e / jnp.sum(e, axis=0, keepdims=True)
        lb = jnp.zeros((1, LANES), F32)
        for j in range(1, layer + 1):
            lb = lb + p[j:j + 1, :]
        return lb

    lb_f = lower_bound(lbl_ref[0:depth, :])
    lb_b = lower_bound(lbl_ref[depth:2 * depth, :])

    ti = lax.broadcasted_iota(jnp.int32, (c, c), 0)
    si = lax.broadcasted_iota(jnp.int32, (c, c), 1)
    tril = si <= ti
    triu = si >= ti
    lane_c = lax.broadcasted_iota(jnp.int32, (c, LANES), 1) < HEAD_DIM
    lane_b = lax.broadcasted_iota(jnp.int32, (rows, LANES), 1) < HEAD_DIM
    er = lax.broadcasted_iota(jnp.int32, (LANES, LANES), 0) < HEAD_DIM
    ec = lax.broadcasted_iota(jnp.int32, (LANES, LANES), 1) < HEAD_DIM
    same_head = er == ec
    lbd = lbd_ref[...]

    def cumsum(logf):
        parts = jnp.concatenate(_split3(logf), axis=1)
        outs = []
        for g in range(rows // group):
            r = _dot(lbd, parts[g * group:(g + 1) * group, :])
            outs.append(r[:, 0:LANES] + r[:, LANES:2 * LANES] + r[:, 2 * LANES:3 * LANES])
        return jnp.concatenate(outs, axis=0)

    def chunk_rows(x, off):
        return jnp.concatenate(
            [jnp.broadcast_to(x[j * c + off:j * c + off + 1, :], (c, LANES)) for j in range(n_chunks)], axis=0)

    def direction(q_ref, z_ref, v_ref, lb, st_ref, o_ref, forward):
        q = q_ref[...].astype(F32)
        z = z_ref[...]
        v = v_ref[...]
        logf = jnp.log(lb + (1.0 - lb) * jax.nn.sigmoid(z))
        kk = (1.0 - lb) * jax.nn.sigmoid(-z)
        a = cumsum(logf)
        last = chunk_rows(a, c - 1)
        if forward:
            e = a
            mid = chunk_rows(a, c // 2 - 1)
            q_in = q * jnp.exp(a)
            k_st = kk * jnp.exp(last - a)
            tri = tril
        else:
            e = a - logf
            mid = chunk_rows(e, c // 2)
            q_in = q * jnp.exp(last - e)
            k_st = kk * jnp.exp(e)
            tri = triu
        sgn = 1.0 if forward else -1.0
        qd = q * jnp.exp(jnp.minimum(sgn * (e - mid), EXP_CLAMP))
        kd = (kk * jnp.exp(jnp.minimum(sgn * (mid - e), EXP_CLAMP))).astype(BF16)
        q0 = jnp.where(lane_b, qd, 0.0).astype(BF16)
        q1 = jnp.where(lane_b, 0.0, qd).astype(BF16)
        q_in = q_in.astype(BF16)
        k_st = k_st.astype(BF16)
        rs = [slice(j * c, (j + 1) * c) for j in range(n_chunks)]
        s0 = [_dot_nt(q0[r], kd[r]) for r in rs]
        s1 = [_dot_nt(q1[r], kd[r]) for r in rs]
        ut = [_dot_tn(v[r], k_st[r]) for r in rs]
        s0 = [jnp.where(tri, s, 0.0).astype(BF16) for s in s0]
        s1 = [jnp.where(tri, s, 0.0).astype(BF16) for s in s1]
        intra = [jnp.where(lane_c, _dot(x0, v[r]), _dot(x1, v[r])) for x0, x1, r in zip(s0, s1, rs)]
        st = st_ref[...]
        states = [None] * n_chunks
        for j in (range(n_chunks) if forward else reversed(range(n_chunks))):
            states[j] = st.astype(BF16)
            st = st * jnp.exp(a[(j + 1) * c - 1:(j + 1) * c, :]) + jnp.where(same_head, ut[j], 0.0)
        st_ref[...] = st
        for j, r in enumerate(rs):
            o_ref[r, :] = _dot_nt(q_in[r], states[j]) + intra[j]

    direction(qf_ref, zf_ref, vf_ref, lb_f, sf_ref, of_ref, True)
    direction(qb_ref, zb_ref, vb_ref, lb_b, sb_ref, ob_ref, False)


def _hgrn(lb_logits2, q, zf, zb, v, batch, seq, layer, rows):
    t = q.shape[0]
    depth = lb_logits2.shape[0] // 2
    n = seq // rows
    fwd = lambda b, p, i: (b * n + i, p)
    bwd = lambda b, p, i: (b * n + (n - 1 - i), p)
    kern = functools.partial(_hgrn_kernel, depth=depth, layer=layer, n_chunks=rows // HGRN_CHUNK)
    spec = lambda im: pl.BlockSpec((rows, LANES), im)
    group = min(256, rows)
    idx = jnp.arange(group)
    lbd = ((idx[None, :] <= idx[:, None])
           & (idx[None, :] // HGRN_CHUNK == idx[:, None] // HGRN_CHUNK)).astype(BF16)
    return pl.pallas_call(
        kern,
        grid=(batch, B_WIDTH // LANES, n),
        in_specs=[pl.BlockSpec((2 * depth, LANES), lambda b, p, i: (0, p)),
                  pl.BlockSpec((group, group), lambda b, p, i: (0, 0)),
                  spec(fwd), spec(fwd), spec(fwd), spec(bwd), spec(bwd), spec(bwd)],
        out_specs=[spec(fwd), spec(bwd)],
        out_shape=[jax.ShapeDtypeStruct((t, B_WIDTH), F32)] * 2,
        scratch_shapes=[pltpu.VMEM((LANES, LANES), F32), pltpu.VMEM((LANES, LANES), F32)],
        compiler_params=_cparams(("parallel", "parallel", "arbitrary")),
        name="hgrn2",
    )(lb_logits2, lbd, q, zf, v, q, zb, v)


def _diff_kernel(lam_ref, g_ref, qc_ref, qn_ref, k_ref, vt_ref, o_ref, s1_ref, s2_ref, m_ref, *,
                 seq, tq, kc, unroll, lam_init):
    lp = lam_ref[...]
    lam = (jnp.exp(jnp.sum(lp[0:1, :] * lp[1:2, :], axis=1, keepdims=True))
           - jnp.exp(jnp.sum(lp[2:3, :] * lp[3:4, :], axis=1, keepdims=True)) + lam_init)
    n = seq // kc
    sub = 8

    def split(qt):
        row = lax.broadcasted_iota(jnp.int32, qt.shape, 0)
        zero = jnp.zeros_like(qt)
        return jnp.where(row < HEAD_DIM, qt, zero), jnp.where(row < HEAD_DIM, zero, qt)

    def scores(k0, q, s_ref, mrun):
        s = _dot(k_ref[pl.ds(k0, kc), :], q)
        s_ref[pl.ds(k0, kc), :] = s
        return jnp.maximum(mrun, jnp.max(s.reshape(kc // sub, sub, tq), axis=0))

    def weights(k0, s_ref, m, lrun, acc):
        p = jnp.exp2(s_ref[pl.ds(k0, kc), :] - m)
        lrun = lrun + jnp.sum(p.reshape(kc // sub, sub, tq), axis=0)
        return lrun, acc + _dot(vt_ref[0, :, pl.ds(k0, kc)], p.astype(BF16))

    part = lambda val: jnp.full((sub, tq), val, F32)
    mat = jnp.zeros((2 * HEAD_DIM, tq), F32)
    loop = functools.partial(lax.fori_loop, 0, n, unroll=unroll)

    @pl.when(pl.program_id(2) == 0)
    def _():
        q1, q2 = split(qc_ref[0])

        def first(j, carry):
            k0 = pl.multiple_of(j * kc, kc)
            return scores(k0, q1, s1_ref, carry[0]), scores(k0, q2, s2_ref, carry[1])

        mr1, mr2 = loop(first, (part(NEG_INF), part(NEG_INF)))
        m_ref[0:sub, :] = mr1
        m_ref[sub:2 * sub, :] = mr2

    m1 = jnp.max(m_ref[0:sub, :], axis=0, keepdims=True)
    m2 = jnp.max(m_ref[sub:2 * sub, :], axis=0, keepdims=True)
    q1n, q2n = split(qn_ref[0])

    def body(j, carry):
        l1, a1, l2, a2, mr1, mr2 = carry
        k0 = pl.multiple_of(j * kc, kc)
        l1, a1 = weights(k0, s1_ref, m1, l1, a1)
        mr1 = scores(k0, q1n, s1_ref, mr1)
        l2, a2 = weights(k0, s2_ref, m2, l2, a2)
        mr2 = scores(k0, q2n, s2_ref, mr2)
        return l1, a1, l2, a2, mr1, mr2

    l1, a1, l2, a2, mr1, mr2 = loop(body, (part(0.0), mat, part(0.0), mat, part(NEG_INF), part(NEG_INF)))
    m_ref[0:sub, :] = mr1
    m_ref[sub:2 * sub, :] = mr2
    l1 = jnp.sum(l1, axis=0, keepdims=True)
    l2 = jnp.sum(l2, axis=0, keepdims=True)
    o = a1 / l1 - lam * (a2 / l2)
    ms = jnp.mean(o * o, axis=0, keepdims=True)
    y = o * lax.rsqrt(ms + EPS) * g_ref[...] * (1.0 - lam_init)
    o_ref[...] = y.T.astype(o_ref.dtype)


def _diff_attn(lam_p, g_col, qct, kc_nat, vct, batch, seq, layer, tq, kc):
    t = kc_nat.shape[0]
    lam_init = 0.8 - 0.6 * math.exp(-0.3 * layer)
    nq = seq // tq
    kern = functools.partial(_diff_kernel, seq=seq, tq=tq, kc=kc, unroll=min(16, seq // kc), lam_init=lam_init)
    return pl.pallas_call(
        kern,
        grid=(batch, N_HEADS, nq),
        in_specs=[pl.BlockSpec(lam_p.shape, lambda b, h, i: (0, 0)),
                  pl.BlockSpec((2 * HEAD_DIM, 1), lambda b, h, i: (0, 0)),
                  pl.BlockSpec((1, 2 * HEAD_DIM, tq), lambda b, h, i: (b, h, i)),
                  pl.BlockSpec((1, 2 * HEAD_DIM, tq), lambda b, h, i: (b, h, jnp.minimum(i + 1, nq - 1))),
                  pl.BlockSpec((seq, 2 * HEAD_DIM), lambda b, h, i: (b, h)),
                  pl.BlockSpec((1, 2 * HEAD_DIM, seq), lambda b, h, i: (b, h, 0))],
        out_specs=pl.BlockSpec((tq, 2 * HEAD_DIM), lambda b, h, i: (b * nq + i, h)),
        out_shape=jax.ShapeDtypeStruct((t, C_WIDTH), BF16),
        scratch_shapes=[pltpu.VMEM((seq, tq), F32), pltpu.VMEM((seq, tq), F32), pltpu.VMEM((16, tq), F32)],
        compiler_params=_cparams(("parallel", "parallel", "arbitrary")),
        name="diff_attn",
    )(lam_p, g_col, qct, qct, kc_nat, vct)


def _out_proj_kernel(x_ref, o1_ref, l1_ref, o2_ref, l2_ref, o3_ref, l3_ref,
                     of_ref, ob_ref, gb_ref, oc_ref, hg_ref, w_ref, g_ref, h_ref, *stage_refs):
    tm = x_ref.shape[0]

    def natural(ref, dil, stage_ref):
        halves = range(A_WIDTH // LANES)
        for r in range(dil):
            blk = ref[0, r]
            for h in halves:
                stage_ref[h, pl.ds(r, tm // dil, stride=dil), :] = blk[:, h * LANES:(h + 1) * LANES]
        return jnp.concatenate([stage_ref[h] for h in halves], axis=1)

    o2, l2 = natural(o2_ref, DILATIONS[1], stage_refs[0]), natural(l2_ref, DILATIONS[1], stage_refs[1])
    o3, l3 = natural(o3_ref, DILATIONS[2], stage_refs[2]), natural(l3_ref, DILATIONS[2], stage_refs[3])
    l1 = l1_ref[...]
    mx = jnp.maximum(jnp.maximum(l1, l2), l3)
    w1, w2, w3 = jnp.exp(l1 - mx), jnp.exp(l2 - mx), jnp.exp(l3 - mx)
    oa = (w1 * o1_ref[...] + w2 * o2 + w3 * o3) / (w1 + w2 + w3)

    y = of_ref[...] + ob_ref[...]
    y2 = y * y
    lane = lax.broadcasted_iota(jnp.int32, y.shape, 1)
    ms = jnp.zeros_like(y)
    for h in range(N_HEADS):
        in_head = (lane >= h * HEAD_DIM) & (lane < (h + 1) * HEAD_DIM)
        ms_h = jnp.sum(jnp.where(in_head, y2, 0.0), axis=-1, keepdims=True) * (1.0 / HEAD_DIM)
        ms = jnp.where(in_head, ms_h, ms)
    gb = gb_ref[...]
    ob = y * lax.rsqrt(ms + EPS) * hg_ref[...] * (gb * jax.nn.sigmoid(gb))

    cat = jnp.concatenate([oa.astype(BF16), ob.astype(BF16), oc_ref[...]], axis=1)
    mix = _dot(cat, w_ref[...])
    h_ref[...] = x_ref[...] + _rms(mix, g_ref[...])


def _out_proj(x2, a_outs, of, ob, gb, oc, hg, w, g, seq, tm):
    t, d = x2.shape
    tiles_per_seq = seq // tm
    row = lambda i: (i, 0)
    const = lambda i: (0, 0)
    r256 = pl.BlockSpec((tm, 256), row)
    sub = lambda dil: pl.BlockSpec((1, dil, tm // dil, A_WIDTH),
                                   lambda i: (i // tiles_per_seq, 0, i % tiles_per_seq, 0))
    a_specs = [r256, r256] + [sub(DILATIONS[1])] * 2 + [sub(DILATIONS[2])] * 2
    return pl.pallas_call(
        _out_proj_kernel,
        grid=(t // tm,),
        in_specs=[pl.BlockSpec((tm, d), row)] + a_specs + [r256] * 3 + [
            pl.BlockSpec((tm, C_WIDTH), row), pl.BlockSpec((1, B_WIDTH), const),
            pl.BlockSpec(w.shape, const), pl.BlockSpec((1, d), const)],
        out_specs=pl.BlockSpec((tm, d), row),
        out_shape=jax.ShapeDtypeStruct((t, d), F32),
        scratch_shapes=[pltpu.VMEM((A_WIDTH // LANES, tm, LANES), F32)] * 4,
        compiler_params=_cparams(("parallel",)),
        name="out_proj",
    )(x2, *a_outs, of, ob, gb, oc, hg, w, g)


def _ffn_kernel(hp_ref, h_ref, hn_ref, gpre_ref, wup_ref, cw_ref, cb_ref, wdn_ref, gpost_ref,
                o_ref, xe_ref, act_ref, *, tm, tiles_per_seq, d_ff, cn):
    i = pl.program_id(0)
    halo = BF16_ROWS
    first = (i % tiles_per_seq) == 0
    last = (i % tiles_per_seq) == tiles_per_seq - 1
    gpre = gpre_ref[...]
    h = h_ref[...]
    xe_ref[halo:halo + tm, :] = _rms(h, gpre).astype(BF16)
    xe_ref[0:halo, :] = jnp.where(first, 0.0, _rms(hp_ref[...], gpre)).astype(BF16)
    xe_ref[halo + tm:2 * halo + tm, :] = jnp.where(last, 0.0, _rms(hn_ref[...], gpre)).astype(BF16)
    xe = xe_ref[...]

    def conv(c0):
        u = _dot(xe, wup_ref[:, c0:c0 + cn])
        cw = cw_ref[:, c0:c0 + cn]
        return (u[halo - 1:halo - 1 + tm] * cw[0:1] + u[halo:halo + tm] * cw[1:2]
                + u[halo + 1:halo + 1 + tm] * cw[2:3] + cb_ref[:, c0:c0 + cn])

    for c in range(d_ff // cn):
        gate = conv(c * cn)
        val = conv(d_ff + c * cn)
        act_ref[:, c * cn:(c + 1) * cn] = (gate * jax.nn.sigmoid(gate) * val).astype(BF16)
    ff = _dot(act_ref[...], wdn_ref[...])
    o_ref[...] = h + _rms(ff, gpost_ref[...])


def _ffn(h2, gpre, wup, cw, cb, wdn, gpost, seq, tm, cn):
    t, d = h2.shape
    d_ff = wdn.shape[0]
    halo = BF16_ROWS
    tiles_per_seq = seq // tm
    hb = tm // halo
    const = lambda i: (0, 0)
    kern = functools.partial(_ffn_kernel, tm=tm, tiles_per_seq=tiles_per_seq, d_ff=d_ff, cn=cn)
    return pl.pallas_call(
        kern,
        grid=(t // tm,),
        in_specs=[pl.BlockSpec((halo, d), lambda i: (jnp.maximum(i * hb - 1, 0), 0)),
                  pl.BlockSpec((tm, d), lambda i: (i, 0)),
                  pl.BlockSpec((halo, d), lambda i: (jnp.minimum((i + 1) * hb, t // halo - 1), 0)),
                  pl.BlockSpec((1, d), const),
                  pl.BlockSpec(wup.shape, const), pl.BlockSpec(cw.shape, const),
                  pl.BlockSpec(cb.shape, const), pl.BlockSpec(wdn.shape, const),
                  pl.BlockSpec((1, d), const)],
        out_specs=pl.BlockSpec((tm, d), lambda i: (i, 0)),
        out_shape=jax.ShapeDtypeStruct((t, d), F32),
        scratch_shapes=[pltpu.VMEM((tm + 2 * halo, d), BF16), pltpu.VMEM((tm, d_ff), BF16)],
        compiler_params=_cparams(("parallel",)),
        name="conv_ffn",
    )(h2, h2, h2, gpre, wup, cw, cb, wdn, gpost)


def _rope_tables(seq):
    pos = jnp.arange(seq, dtype=F32)
    inv = ROPE_THETA ** (-jnp.arange(0, ROPE_DIM, 2, dtype=F32) / ROPE_DIM)
    ang = pos[:, None] * inv[None, :]
    cos, sin = jnp.cos(ang), jnp.sin(ang)
    rest = HEAD_DIM - ROPE_DIM
    one, zero = jnp.ones((seq, rest), F32), jnp.zeros((seq, rest), F32)
    zh = jnp.zeros((seq, ROPE_HALF), F32)
    rep = LANES // HEAD_DIM
    rc = jnp.tile(jnp.concatenate([cos, cos, one], axis=1), (1, rep))
    rs1 = jnp.tile(jnp.concatenate([-sin, zh, zero], axis=1), (1, rep))
    rs2 = jnp.tile(jnp.concatenate([zh, sin, zero], axis=1), (1, rep))
    return rc, rs1, rs2


def kernel(x, w_in, w_out, lb_logits, hgrn_norm, diff_lambda, diff_norm, w_up, conv_w, conv_b,
           w_down, norm_pre_mix, norm_post_mix, norm_pre_ffn, norm_post_ffn):
    batch, seq, d = x.shape
    depth = w_in.shape[0]
    tm = min(512, seq)
    rc, rs1, rs2 = _rope_tables(seq)
    lbl2 = lb_logits.astype(F32).reshape(2 * depth, B_WIDTH)
    x2 = x.reshape(batch * seq, d)
    for l in range(depth):
        (qa, ka, va, qa4, ka4, va4, qa16, ka16, va16, qb, zf, zb, ib, gb, qct, kc, vct) = _in_proj(
            x2, norm_pre_mix[l].reshape(1, d), w_in[l].astype(BF16), rc, rs1, rs2, batch, seq, tm)

        a_outs = []
        for dil, qkv in zip(DILATIONS, ((qa, ka, va), (qa4, ka4, va4), (qa16, ka16, va16))):
            o, lse = _banded(*(a.reshape(batch * dil, seq // dil, A_WIDTH) for a in qkv))
            shape = (batch * seq, A_WIDTH) if dil == 1 else (batch, dil, seq // dil, A_WIDTH)
            a_outs += [o.reshape(shape), lse.reshape(shape)]

        of, ob = _hgrn(lbl2, qb, zf, zb, ib, batch, seq, l, rows=min(512, seq))

        oc = _diff_attn(diff_lambda[l].astype(F32), diff_norm[l].astype(F32).reshape(2 * HEAD_DIM, 1),
                        qct, kc, vct, batch, seq, l, tq=min(256, seq), kc=min(512, seq))

        hg = jnp.tile(hgrn_norm[l].astype(F32), N_HEADS).reshape(1, B_WIDTH)
        h2 = _out_proj(x2, a_outs, of, ob, gb, oc, hg, w_out[l].astype(BF16),
                       norm_post_mix[l].reshape(1, d), seq, tm)

        x2 = _ffn(h2, norm_pre_ffn[l].reshape(1, d), w_up[l].astype(BF16), conv_w[l],
                  conv_b[l].reshape(1, -1), w_down[l].astype(BF16), norm_post_ffn[l].reshape(1, d),
                  seq, tm, cn=256)
    return x2.reshape(batch, seq, d)
```

```python
import functools
import math

import jax
import jax.numpy as jnp
from jax import lax
from jax.experimental import pallas as pl
from jax.experimental.pallas import tpu as pltpu

F32 = jnp.float32
BF16 = jnp.bfloat16

HEAD_DIM = 64
N_HEADS = 4
A_WIDTH = N_HEADS * HEAD_DIM
B_WIDTH = N_HEADS * HEAD_DIM
C_WIDTH = N_HEADS * 2 * HEAD_DIM
ROPE_THETA = 500000.0
ROPE_DIM = HEAD_DIM // 4
ROPE_HALF = ROPE_DIM // 2
DILATIONS = (1, 4, 16)
BAND_HALF = 64
HGRN_CHUNK = 64
CONV_WIDTH = 3
EPS = 1e-6
NEG_INF = -1e30
LOG2E = math.log2(math.e)
EXP_CLAMP = 80.0

LANES = 128
BF16_ROWS = 16
VMEM_LIMIT = 56 * 1024 * 1024

OFF_QA, OFF_KA, OFF_VA = 0, 256, 512
OFF_QB, OFF_ZF, OFF_ZB, OFF_IB, OFF_GB = 768, 1024, 1280, 1536, 1792
OFF_QC, OFF_KC, OFF_VC = 2048, 2560, 3072


def _cparams(sem, flags=None):
    return pltpu.CompilerParams(dimension_semantics=sem, vmem_limit_bytes=VMEM_LIMIT, flags=flags)


def _rms(x, g):
    ms = jnp.mean(x * x, axis=-1, keepdims=True)
    return x * lax.rsqrt(ms + EPS) * g


def _dot(a, b):
    return jnp.dot(a, b, preferred_element_type=F32)


def _dot_nt(a, b):
    return lax.dot_general(a, b, (((1,), (1,)), ((), ())), preferred_element_type=F32)


def _dot_tn(a, b):
    return lax.dot_general(a, b, (((0,), (0,)), ((), ())), preferred_element_type=F32)


def _in_proj_kernel(x_ref, g_ref, w_ref, rc_ref, rs1_ref, rs2_ref,
                    qa_ref, ka_ref, va_ref, qa4_ref, ka4_ref, va4_ref, qa16_ref, ka16_ref, va16_ref,
                    qb_ref, zf_ref, zb_ref, ib_ref, gb_ref, qct_ref, kc_ref, vct_ref,
                    sq_ref, sk_ref, sv_ref):
    tm = x_ref.shape[0]
    xn = _rms(x_ref[...], g_ref[...]).astype(BF16)

    def emit_a(val, stage_ref, nat_ref, sub_refs):
        nat_ref[...] = val.astype(BF16)
        halves = range(A_WIDTH // LANES)
        for h in halves:
            stage_ref[h] = val[:, h * LANES:(h + 1) * LANES]
        for dil, ref in zip(DILATIONS[1:], sub_refs):
            for r in range(dil):
                ref[0, r, :, :] = jnp.concatenate(
                    [stage_ref[h, pl.ds(r, tm // dil, stride=dil), :] for h in halves], axis=1).astype(BF16)

    rc, rs1, rs2 = rc_ref[...], rs1_ref[...], rs2_ref[...]

    def proj(c0, n):
        return _dot(xn, w_ref[:, c0:c0 + n])

    def rope(a):
        outs = []
        for j in range(a.shape[1] // LANES):
            blk = a[:, j * LANES:(j + 1) * LANES]
            outs.append(blk * rc + pltpu.roll(blk, LANES - ROPE_HALF, 1) * rs1
                        + pltpu.roll(blk, ROPE_HALF, 1) * rs2)
        return outs

    def cat(blks):
        return jnp.concatenate(blks, axis=1)

    scale = 1.0 / math.sqrt(HEAD_DIM)
    emit_a(cat(rope(proj(OFF_QA, 256))) * scale, sq_ref, qa_ref, (qa4_ref, qa16_ref))
    emit_a(cat(rope(proj(OFF_KA, 256))), sk_ref, ka_ref, (ka4_ref, ka16_ref))
    emit_a(proj(OFF_VA, 256), sv_ref, va_ref, (va4_ref, va16_ref))
    qb_ref[...] = proj(OFF_QB, 256).astype(BF16)
    zf_ref[...] = proj(OFF_ZF, 256)
    zb_ref[...] = proj(OFF_ZB, 256)
    ib_ref[...] = proj(OFF_IB, 256).astype(BF16)
    gb_ref[...] = proj(OFF_GB, 256)
    for half in range(2):
        qblks = rope(proj(OFF_QC + 256 * half, 256))
        kblks = rope(proj(OFF_KC + 256 * half, 256))
        v = proj(OFF_VC + 256 * half, 256)
        kc_ref[:, 256 * half:256 * (half + 1)] = cat(kblks).astype(BF16)
        for j in range(2):
            r0 = 256 * half + LANES * j
            qct_ref[0, r0:r0 + LANES, :] = (qblks[j] * (scale * LOG2E)).T.astype(BF16)
            vct_ref[0, r0:r0 + LANES, :] = v[:, j * LANES:(j + 1) * LANES].T.astype(BF16)


def _in_proj(x2, g, w, rc, rs1, rs2, batch, seq, tm):
    t, d = x2.shape
    n_cols = w.shape[1]
    tiles_per_seq = seq // tm
    row = lambda i: (i, 0)
    const = lambda i: (0, 0)
    ropei = lambda i: (i % tiles_per_seq, 0)
    tr = lambda i: (i // tiles_per_seq, 0, i % tiles_per_seq)
    nat = lambda n, dt: jax.ShapeDtypeStruct((t, n), dt)
    sub_shape = lambda dil: jax.ShapeDtypeStruct((batch, dil, seq // dil, A_WIDTH), BF16)
    sub_spec = lambda dil: pl.BlockSpec((1, dil, tm // dil, A_WIDTH),
                                        lambda i: (i // tiles_per_seq, 0, i % tiles_per_seq, 0))
    out_shape = ([nat(256, BF16)] * 3
                 + [sub_shape(DILATIONS[1])] * 3 + [sub_shape(DILATIONS[2])] * 3
                 + [nat(256, BF16), nat(256, F32), nat(256, F32), nat(256, BF16), nat(256, F32)]
                 + [jax.ShapeDtypeStruct((batch, C_WIDTH, seq), BF16),
                    nat(C_WIDTH, BF16),
                    jax.ShapeDtypeStruct((batch, C_WIDTH, seq), BF16)])
    out_specs = ([pl.BlockSpec((tm, 256), row)] * 3 + [sub_spec(DILATIONS[1])] * 3 + [sub_spec(DILATIONS[2])] * 3
                 + [pl.BlockSpec((tm, 256), row)] * 5
                 + [pl.BlockSpec((1, C_WIDTH, tm), tr), pl.BlockSpec((tm, C_WIDTH), row),
                    pl.BlockSpec((1, C_WIDTH, tm), tr)])
    return pl.pallas_call(
        _in_proj_kernel,
        grid=(t // tm,),
        in_specs=[pl.BlockSpec((tm, d), row), pl.BlockSpec((1, d), const),
                  pl.BlockSpec((d, n_cols), const),
                  pl.BlockSpec((tm, LANES), ropei), pl.BlockSpec((tm, LANES), ropei),
                  pl.BlockSpec((tm, LANES), ropei)],
        out_specs=out_specs,
        out_shape=out_shape,
        scratch_shapes=[pltpu.VMEM((A_WIDTH // LANES, tm, LANES), F32)] * 3,
        compiler_params=_cparams(("parallel",)),
        name="in_proj",
    )(x2, g, w, rc, rs1, rs2)


def _banded_kernel(q_ref, k_ref, v_ref, o_ref, lse_ref, *, seq_len, q_block, sub, k_win):
    i = pl.program_id(1)
    rows = N_HEADS * sub
    lane = lax.broadcasted_iota(jnp.int32, (rows, A_WIDTH), 1)
    row_head = lax.broadcasted_iota(jnp.int32, (rows, A_WIDTH), 0) // sub
    own_lanes = (lane // HEAD_DIM) == row_head
    qi = lax.broadcasted_iota(jnp.int32, (rows, k_win), 0) % sub
    ki = lax.broadcasted_iota(jnp.int32, (rows, k_win), 1)
    n_sub = q_block // sub
    q0s = [i * q_block + sb * sub for sb in range(n_sub)]
    kss = [pl.multiple_of(jnp.clip(q0 - BAND_HALF, 0, seq_len - k_win), BAND_HALF) for q0 in q0s]

    def score(sb):
        qblk = q_ref[0, sb * sub:(sb + 1) * sub, :]
        q4 = jnp.concatenate([qblk] * N_HEADS, axis=0)
        q4 = jnp.where(own_lanes, q4, jnp.zeros_like(q4))
        return _dot_nt(q4, k_ref[0, pl.ds(kss[sb], k_win), :])

    def softmax(sb, s):
        band = jnp.abs((kss[sb] + ki) - (q0s[sb] + qi)) <= BAND_HALF
        s = jnp.where(band, s, NEG_INF)
        m = jnp.max(s, axis=-1, keepdims=True)
        p = jnp.exp(s - m)
        den = jnp.sum(p, axis=-1, keepdims=True)
        return p.astype(BF16), den, m + jnp.log(den)

    ss = [score(sb) for sb in range(n_sub)]
    sm = [softmax(sb, s) for sb, s in enumerate(ss)]
    pv = [_dot(p, v_ref[0, pl.ds(kss[sb], k_win), :]) for sb, (p, _, _) in enumerate(sm)]
    for sb in range(n_sub):
        _, den, lse = sm[sb]
        o4 = jnp.where(own_lanes, pv[sb] / den, 0.0)
        l4 = jnp.where(own_lanes, lse, 0.0)
        o_acc, lse_acc = o4[0:sub], l4[0:sub]
        for h in range(1, N_HEADS):
            o_acc = o_acc + o4[h * sub:(h + 1) * sub]
            lse_acc = lse_acc + l4[h * sub:(h + 1) * sub]
        o_ref[0, sb * sub:(sb + 1) * sub, :] = o_acc
        lse_ref[0, sb * sub:(sb + 1) * sub, :] = lse_acc


def _banded(q, k, v):
    n_seq, seq_len, w = q.shape
    q_block = min(512, seq_len)
    sub = min(128, seq_len)
    k_win = min(sub + 2 * BAND_HALF, seq_len)
    kern = functools.partial(_banded_kernel, seq_len=seq_len, q_block=q_block, sub=sub, k_win=k_win)
    full = pl.BlockSpec((1, seq_len, w), lambda s, i: (s, 0, 0))
    blk = pl.BlockSpec((1, q_block, w), lambda s, i: (s, i, 0))
    return pl.pallas_call(
        kern,
        grid=(n_seq, seq_len // q_block),
        in_specs=[blk, full, full],
        out_specs=[blk, blk],
        out_shape=[jax.ShapeDtypeStruct((n_seq, seq_len, w), F32)] * 2,
        compiler_params=_cparams(("parallel", "arbitrary")),
        name="banded_attn",
    )(q, k, v)


def _split3(x):
    hi = x.astype(BF16)
    r1 = x - hi.astype(F32)
    mid = r1.astype(BF16)
    lo = (r1 - mid.astype(F32)).astype(BF16)
    return hi, mid, lo


def _hgrn_kernel(lbl_ref, lbd_ref, qf_ref, zf_ref, vf_ref, qb_ref, zb_ref, vb_ref,
                 of_ref, ob_ref, sf_ref, sb_ref, *, depth, layer, n_chunks):
    c = HGRN_CHUNK
    rows = n_chunks * c
    group = lbd_ref.shape[0]

    @pl.when(pl.program_id(2) == 0)
    def _():
        sf_ref[...] = jnp.zeros_like(sf_ref)
        sb_ref[...] = jnp.zeros_like(sb_ref)

    def lower_bound(logits):
        e = jnp.exp(logits - jnp.max(logits, axis=0, keepdims=True))
        p = e / jnp.sum(e, axis=0, keepdims=True)
        lb = jnp.zeros((1, LANES), F32)
        for j in range(1, layer + 1):
            lb = lb + p[j:j + 1, :]
        return lb

    lb_f = lower_bound(lbl_ref[0:depth, :])
    lb_b = lower_bound(lbl_ref[depth:2 * depth, :])

    ti = lax.broadcasted_iota(jnp.int32, (c, c), 0)
    si = lax.broadcasted_iota(jnp.int32, (c, c), 1)
    tril = si <= ti
    triu = si >= ti
    lane_c = lax.broadcasted_iota(jnp.int32, (c, LANES), 1) < HEAD_DIM
    lane_b = lax.broadcasted_iota(jnp.int32, (rows, LANES), 1) < HEAD_DIM
    er = lax.broadcasted_iota(jnp.int32, (LANES, LANES), 0) < HEAD_DIM
    ec = lax.broadcasted_iota(jnp.int32, (LANES, LANES), 1) < HEAD_DIM
    same_head = er == ec
    lbd = lbd_ref[...]

    def cumsum(logf):
        parts = jnp.concatenate(_split3(logf), axis=1)
        outs = []
        for g in range(rows // group):
            r = _dot(lbd, parts[g * group:(g + 1) * group, :])
            outs.append(r[:, 0:LANES] + r[:, LANES:2 * LANES] + r[:, 2 * LANES:3 * LANES])
        return jnp.concatenate(outs, axis=0)

    def chunk_rows(x, off):
        return jnp.concatenate(
            [jnp.broadcast_to(x[j * c + off:j * c + off + 1, :], (c, LANES)) for j in range(n_chunks)], axis=0)

    def direction(q_ref, z_ref, v_ref, lb, st_ref, o_ref, forward):
        q = q_ref[...].astype(F32)
        z = z_ref[...]
        v = v_ref[...]
        logf = jnp.log(lb + (1.0 - lb) * jax.nn.sigmoid(z))
        kk = (1.0 - lb) * jax.nn.sigmoid(-z)
        a = cumsum(logf)
        last = chunk_rows(a, c - 1)
        if forward:
            e = a
            mid = chunk_rows(a, c // 2 - 1)
            q_in = q * jnp.exp(a)
            k_st = kk * jnp.exp(last - a)
            tri = tril
        else:
            e = a - logf
            mid = chunk_rows(e, c // 2)
            q_in = q * jnp.exp(last - e)
            k_st = kk * jnp.exp(e)
            tri = triu
        sgn = 1.0 if forward else -1.0
        qd = q * jnp.exp(jnp.minimum(sgn * (e - mid), EXP_CLAMP))
        kd = (kk * jnp.exp(jnp.minimum(sgn * (mid - e), EXP_CLAMP))).astype(BF16)
        q0 = jnp.where(lane_b, qd, 0.0).astype(BF16)
        q1 = jnp.where(lane_b, 0.0, qd).astype(BF16)
        q_in = q_in.astype(BF16)
        k_st = k_st.astype(BF16)
        rs = [slice(j * c, (j + 1) * c) for j in range(n_chunks)]
        s0 = [_dot_nt(q0[r], kd[r]) for r in rs]
        s1 = [_dot_nt(q1[r], kd[r]) for r in rs]
        ut = [_dot_tn(v[r], k_st[r]) for r in rs]
        s0 = [jnp.where(tri, s, 0.0).astype(BF16) for s in s0]
        s1 = [jnp.where(tri, s, 0.0).astype(BF16) for s in s1]
        intra = [jnp.where(lane_c, _dot(x0, v[r]), _dot(x1, v[r])) for x0, x1, r in zip(s0, s1, rs)]
        st = st_ref[...]
        states = [None] * n_chunks
        for j in (range(n_chunks) if forward else reversed(range(n_chunks))):
            states[j] = st.astype(BF16)
            st = st * jnp.exp(a[(j + 1) * c - 1:(j + 1) * c, :]) + jnp.where(same_head, ut[j], 0.0)
        st_ref[...] = st
        for j, r in enumerate(rs):
            o_ref[r, :] = _dot_nt(q_in[r], states[j]) + intra[j]

    direction(qf_ref, zf_ref, vf_ref, lb_f, sf_ref, of_ref, True)
    direction(qb_ref, zb_ref, vb_ref, lb_b, sb_ref, ob_ref, False)


def _hgrn(lb_logits2, q, zf, zb, v, batch, seq, layer, rows):
    t = q.shape[0]
    depth = lb_logits2.shape[0] // 2
    n = seq // rows
    fwd = lambda b, p, i: (b * n + i, p)
    bwd = lambda b, p, i: (b * n + (n - 1 - i), p)
    kern = functools.partial(_hgrn_kernel, depth=depth, layer=layer, n_chunks=rows // HGRN_CHUNK)
    spec = lambda im: pl.BlockSpec((rows, LANES), im)
    group = min(256, rows)
    idx = jnp.arange(group)
    lbd = ((idx[None, :] <= idx[:, None])
           & (idx[None, :] // HGRN_CHUNK == idx[:, None] // HGRN_CHUNK)).astype(BF16)
    return pl.pallas_call(
        kern,
        grid=(batch, B_WIDTH // LANES, n),
        in_specs=[pl.BlockSpec((2 * depth, LANES), lambda b, p, i: (0, p)),
                  pl.BlockSpec((group, group), lambda b, p, i: (0, 0)),
                  spec(fwd), spec(fwd), spec(fwd), spec(bwd), spec(bwd), spec(bwd)],
        out_specs=[spec(fwd), spec(bwd)],
        out_shape=[jax.ShapeDtypeStruct((t, B_WIDTH), F32)] * 2,
        scratch_shapes=[pltpu.VMEM((LANES, LANES), F32), pltpu.VMEM((LANES, LANES), F32)],
        compiler_params=_cparams(("parallel", "parallel", "arbitrary")),
        name="hgrn2",
    )(lb_logits2, lbd, q, zf, v, q, zb, v)


def _diff_kernel(lam_ref, g_ref, q0_ref, q1_ref, q2_ref, k_ref, vt_ref, o_ref,
                 s1_ref, s2_ref, p1_ref, p2_ref, st_ref, *, seq, tq, kc, unroll, lam_init):
    lp = lam_ref[...]
    lam = (jnp.exp(jnp.sum(lp[0:1, :] * lp[1:2, :], axis=1, keepdims=True))
           - jnp.exp(jnp.sum(lp[2:3, :] * lp[3:4, :], axis=1, keepdims=True)) + lam_init)
    n = seq // kc
    sub = 8

    def split(qt):
        row = lax.broadcasted_iota(jnp.int32, qt.shape, 0)
        zero = jnp.zeros_like(qt)
        return jnp.where(row < HEAD_DIM, qt, zero), jnp.where(row < HEAD_DIM, zero, qt)

    def scores(k0, q, s_ref, mrun):
        s = _dot(k_ref[pl.ds(k0, kc), :], q)
        s_ref[pl.ds(k0, kc), :] = s
        return jnp.maximum(mrun, jnp.max(s.reshape(kc // sub, sub, tq), axis=0))

    def expo(k0, s_ref, p_ref, m, lrun):
        p = jnp.exp2(s_ref[pl.ds(k0, kc), :] - m)
        p_ref[pl.ds(k0, kc), :] = p.astype(BF16)
        return lrun + jnp.sum(p.reshape(kc // sub, sub, tq), axis=0)

    part = lambda val: jnp.full((sub, tq), val, F32)
    loop = functools.partial(lax.fori_loop, 0, n, unroll=unroll)
    fill_loop = functools.partial(lax.fori_loop, 0, n, unroll=min(4, n))
    rows = lambda a: slice(a * sub, (a + 1) * sub)
    colmax = lambda a: jnp.max(st_ref[rows(a), :], axis=0, keepdims=True)
    colsum = lambda a: jnp.sum(st_ref[rows(a), :], axis=0, keepdims=True)

    @pl.when(pl.program_id(2) == 0)
    def _():
        qa1, qa2 = split(q0_ref[0])

        def fill_s(j, carry):
            k0 = pl.multiple_of(j * kc, kc)
            return scores(k0, qa1, s1_ref, carry[0]), scores(k0, qa2, s2_ref, carry[1])

        mr1, mr2 = fill_loop(fill_s, (part(NEG_INF), part(NEG_INF)))
        ma1 = jnp.max(mr1, axis=0, keepdims=True)
        ma2 = jnp.max(mr2, axis=0, keepdims=True)
        qb1, qb2 = split(q1_ref[0])

        def fill_e(j, carry):
            l1, l2, mr1, mr2 = carry
            k0 = pl.multiple_of(j * kc, kc)
            l1 = expo(k0, s1_ref, p1_ref, ma1, l1)
            mr1 = scores(k0, qb1, s1_ref, mr1)
            l2 = expo(k0, s2_ref, p2_ref, ma2, l2)
            mr2 = scores(k0, qb2, s2_ref, mr2)
            return l1, l2, mr1, mr2

        l1, l2, mr1, mr2 = fill_loop(fill_e, (part(0.0), part(0.0), part(NEG_INF), part(NEG_INF)))
        st_ref[rows(0), :] = mr1
        st_ref[rows(1), :] = mr2
        st_ref[rows(2), :] = l1
        st_ref[rows(3), :] = l2

    m1, m2 = colmax(0), colmax(1)
    l1, l2 = colsum(2), colsum(3)
    c16 = jnp.broadcast_to((lam * l1 / l2).astype(BF16), (BF16_ROWS, tq))
    qs1, qs2 = split(q2_ref[0])

    def body(j, carry):
        acc, e1, e2, mr1, mr2 = carry
        k0 = pl.multiple_of(j * kc, kc)
        p2 = p2_ref[pl.ds(k0, kc), :].reshape(kc // BF16_ROWS, BF16_ROWS, tq)
        w = p1_ref[pl.ds(k0, kc), :] - (p2 * c16).reshape(kc, tq)
        acc = acc + _dot(vt_ref[0, :, pl.ds(k0, kc)], w)
        e1 = expo(k0, s1_ref, p1_ref, m1, e1)
        mr1 = scores(k0, qs1, s1_ref, mr1)
        e2 = expo(k0, s2_ref, p2_ref, m2, e2)
        mr2 = scores(k0, qs2, s2_ref, mr2)
        return acc, e1, e2, mr1, mr2

    acc, e1, e2, mr1, mr2 = loop(body, (jnp.zeros((2 * HEAD_DIM, tq), F32), part(0.0), part(0.0),
                                        part(NEG_INF), part(NEG_INF)))
    st_ref[rows(0), :] = mr1
    st_ref[rows(1), :] = mr2
    st_ref[rows(2), :] = e1
    st_ref[rows(3), :] = e2
    o = acc / l1
    ms = jnp.mean(o * o, axis=0, keepdims=True)
    y = o * lax.rsqrt(ms + EPS) * g_ref[...] * (1.0 - lam_init)
    o_ref[...] = y.T.astype(o_ref.dtype)


def _diff_attn(lam_p, g_col, qct, kc_nat, vct, batch, seq, layer, tq, kc):
    t = kc_nat.shape[0]
    lam_init = 0.8 - 0.6 * math.exp(-0.3 * layer)
    nq = seq // tq
    kern = functools.partial(_diff_kernel, seq=seq, tq=tq, kc=kc, unroll=min(16, seq // kc), lam_init=lam_init)
    q_spec = lambda ahead: pl.BlockSpec((1, 2 * HEAD_DIM, tq),
                                        lambda b, h, i: (b, h, jnp.minimum(i + ahead, nq - 1)))
    return pl.pallas_call(
        kern,
        grid=(batch, N_HEADS, nq),
        in_specs=[pl.BlockSpec(lam_p.shape, lambda b, h, i: (0, 0)),
                  pl.BlockSpec((2 * HEAD_DIM, 1), lambda b, h, i: (0, 0)),
                  q_spec(0), q_spec(1), q_spec(2),
                  pl.BlockSpec((seq, 2 * HEAD_DIM), lambda b, h, i: (b, h)),
                  pl.BlockSpec((1, 2 * HEAD_DIM, seq), lambda b, h, i: (b, h, 0))],
        out_specs=pl.BlockSpec((tq, 2 * HEAD_DIM), lambda b, h, i: (b * nq + i, h)),
        out_shape=jax.ShapeDtypeStruct((t, C_WIDTH), BF16),
        scratch_shapes=[pltpu.VMEM((seq, tq), F32), pltpu.VMEM((seq, tq), F32),
                        pltpu.VMEM((seq, tq), BF16), pltpu.VMEM((seq, tq), BF16),
                        pltpu.VMEM((32, tq), F32)],
        compiler_params=_cparams(("parallel", "parallel", "arbitrary")),
        name="diff_attn",
    )(lam_p, g_col, qct, qct, qct, kc_nat, vct)


def _out_proj_kernel(x_ref, o1_ref, l1_ref, o2_ref, l2_ref, o3_ref, l3_ref,
                     of_ref, ob_ref, gb_ref, oc_ref, hg_ref, w_ref, g_ref, h_ref, *stage_refs):
    tm = x_ref.shape[0]

    def natural(ref, dil, stage_ref):
        halves = range(A_WIDTH // LANES)
        for r in range(dil):
            blk = ref[0, r]
            for h in halves:
                stage_ref[h, pl.ds(r, tm // dil, stride=dil), :] = blk[:, h * LANES:(h + 1) * LANES]
        return jnp.concatenate([stage_ref[h] for h in halves], axis=1)

    o2, l2 = natural(o2_ref, DILATIONS[1], stage_refs[0]), natural(l2_ref, DILATIONS[1], stage_refs[1])
    o3, l3 = natural(o3_ref, DILATIONS[2], stage_refs[2]), natural(l3_ref, DILATIONS[2], stage_refs[3])
    l1 = l1_ref[...]
    mx = jnp.maximum(jnp.maximum(l1, l2), l3)
    w1, w2, w3 = jnp.exp(l1 - mx), jnp.exp(l2 - mx), jnp.exp(l3 - mx)
    oa = (w1 * o1_ref[...] + w2 * o2 + w3 * o3) / (w1 + w2 + w3)

    y = of_ref[...] + ob_ref[...]
    y2 = y * y
    lane = lax.broadcasted_iota(jnp.int32, y.shape, 1)
    ms = jnp.zeros_like(y)
    for h in range(N_HEADS):
        in_head = (lane >= h * HEAD_DIM) & (lane < (h + 1) * HEAD_DIM)
        ms_h = jnp.sum(jnp.where(in_head, y2, 0.0), axis=-1, keepdims=True) * (1.0 / HEAD_DIM)
        ms = jnp.where(in_head, ms_h, ms)
    gb = gb_ref[...]
    ob = y * lax.rsqrt(ms + EPS) * hg_ref[...] * (gb * jax.nn.sigmoid(gb))

    cat = jnp.concatenate([oa.astype(BF16), ob.astype(BF16), oc_ref[...]], axis=1)
    mix = _dot(cat, w_ref[...])
    h_ref[...] = x_ref[...] + _rms(mix, g_ref[...])


def _out_proj(x2, a_outs, of, ob, gb, oc, hg, w, g, seq, tm):
    t, d = x2.shape
    tiles_per_seq = seq // tm
    row = lambda i: (i, 0)
    const = lambda i: (0, 0)
    r256 = pl.BlockSpec((tm, 256), row)
    sub = lambda dil: pl.BlockSpec((1, dil, tm // dil, A_WIDTH),
                                   lambda i: (i // tiles_per_seq, 0, i % tiles_per_seq, 0))
    a_specs = [r256, r256] + [sub(DILATIONS[1])] * 2 + [sub(DILATIONS[2])] * 2
    return pl.pallas_call(
        _out_proj_kernel,
        grid=(t // tm,),
        in_specs=[pl.BlockSpec((tm, d), row)] + a_specs + [r256] * 3 + [
            pl.BlockSpec((tm, C_WIDTH), row), pl.BlockSpec((1, B_WIDTH), const),
            pl.BlockSpec(w.shape, const), pl.BlockSpec((1, d), const)],
        out_specs=pl.BlockSpec((tm, d), row),
        out_shape=jax.ShapeDtypeStruct((t, d), F32),
        scratch_shapes=[pltpu.VMEM((A_WIDTH // LANES, tm, LANES), F32)] * 4,
        compiler_params=_cparams(("parallel",)),
        name="out_proj",
    )(x2, *a_outs, of, ob, gb, oc, hg, w, g)


def _ffn_kernel(hp_ref, h_ref, hn_ref, gpre_ref, wup_ref, cw_ref, cb_ref, wdn_ref, gpost_ref,
                o_ref, xe_ref, act_ref, *, tm, tiles_per_seq, d_ff, cn):
    i = pl.program_id(0)
    halo = BF16_ROWS
    first = (i % tiles_per_seq) == 0
    last = (i % tiles_per_seq) == tiles_per_seq - 1
    gpre = gpre_ref[...]
    h = h_ref[...]
    xe_ref[halo:halo + tm, :] = _rms(h, gpre).astype(BF16)
    xe_ref[0:halo, :] = jnp.where(first, 0.0, _rms(hp_ref[...], gpre)).astype(BF16)
    xe_ref[halo + tm:2 * halo + tm, :] = jnp.where(last, 0.0, _rms(hn_ref[...], gpre)).astype(BF16)
    xe = xe_ref[...]

    def conv(c0):
        u = _dot(xe, wup_ref[:, c0:c0 + cn])
        cw = cw_ref[:, c0:c0 + cn]
        return (u[halo - 1:halo - 1 + tm] * cw[0:1] + u[halo:halo + tm] * cw[1:2]
                + u[halo + 1:halo + 1 + tm] * cw[2:3] + cb_ref[:, c0:c0 + cn])

    for c in range(d_ff // cn):
        gate = conv(c * cn)
        val = conv(d_ff + c * cn)
        act_ref[:, c * cn:(c + 1) * cn] = (gate * jax.nn.sigmoid(gate) * val).astype(BF16)
    ff = _dot(act_ref[...], wdn_ref[...])
    o_ref[...] = h + _rms(ff, gpost_ref[...])


def _ffn(h2, gpre, wup, cw, cb, wdn, gpost, seq, tm, cn):
    t, d = h2.shape
    d_ff = wdn.shape[0]
    halo = BF16_ROWS
    tiles_per_seq = seq // tm
    hb = tm // halo
    const = lambda i: (0, 0)
    kern = functools.partial(_ffn_kernel, tm=tm, tiles_per_seq=tiles_per_seq, d_ff=d_ff, cn=cn)
    return pl.pallas_call(
        kern,
        grid=(t // tm,),
        in_specs=[pl.BlockSpec((halo, d), lambda i: (jnp.maximum(i * hb - 1, 0), 0)),
                  pl.BlockSpec((tm, d), lambda i: (i, 0)),
                  pl.BlockSpec((halo, d), lambda i: (jnp.minimum((i + 1) * hb, t // halo - 1), 0)),
                  pl.BlockSpec((1, d), const),
                  pl.BlockSpec(wup.shape, const), pl.BlockSpec(cw.shape, const),
                  pl.BlockSpec(cb.shape, const), pl.BlockSpec(wdn.shape, const),
                  pl.BlockSpec((1, d), const)],
        out_specs=pl.BlockSpec((tm, d), lambda i: (i, 0)),
        out_shape=jax.ShapeDtypeStruct((t, d), F32),
        scratch_shapes=[pltpu.VMEM((tm + 2 * halo, d), BF16), pltpu.VMEM((tm, d_ff), BF16)],
        compiler_params=_cparams(("parallel",)),
        name="conv_ffn",
    )(h2, h2, h2, gpre, wup, cw, cb, wdn, gpost)


def _rope_tables(seq):
    pos = jnp.arange(seq, dtype=F32)
    inv = ROPE_THETA ** (-jnp.arange(0, ROPE_DIM, 2, dtype=F32) / ROPE_DIM)
    ang = pos[:, None] * inv[None, :]
    cos, sin = jnp.cos(ang), jnp.sin(ang)
    rest = HEAD_DIM - ROPE_DIM
    one, zero = jnp.ones((seq, rest), F32), jnp.zeros((seq, rest), F32)
    zh = jnp.zeros((seq, ROPE_HALF), F32)
    rep = LANES // HEAD_DIM
    rc = jnp.tile(jnp.concatenate([cos, cos, one], axis=1), (1, rep))
    rs1 = jnp.tile(jnp.concatenate([-sin, zh, zero], axis=1), (1, rep))
    rs2 = jnp.tile(jnp.concatenate([zh, sin, zero], axis=1), (1, rep))
    return rc, rs1, rs2


def kernel(x, w_in, w_out, lb_logits, hgrn_norm, diff_lambda, diff_norm, w_up, conv_w, conv_b,
           w_down, norm_pre_mix, norm_post_mix, norm_pre_ffn, norm_post_ffn):
    batch, seq, d = x.shape
    depth = w_in.shape[0]
    tm = min(512, seq)
    rc, rs1, rs2 = _rope_tables(seq)
    lbl2 = lb_logits.astype(F32).reshape(2 * depth, B_WIDTH)
    x2 = x.reshape(batch * seq, d)
    for l in range(depth):
        (qa, ka, va, qa4, ka4, va4, qa16, ka16, va16, qb, zf, zb, ib, gb, qct, kc, vct) = _in_proj(
            x2, norm_pre_mix[l].reshape(1, d), w_in[l].astype(BF16), rc, rs1, rs2, batch, seq, tm)

        a_outs = []
        for dil, qkv in zip(DILATIONS, ((qa, ka, va), (qa4, ka4, va4), (qa16, ka16, va16))):
            o, lse = _banded(*(a.reshape(batch * dil, seq // dil, A_WIDTH) for a in qkv))
            shape = (batch * seq, A_WIDTH) if dil == 1 else (batch, dil, seq // dil, A_WIDTH)
            a_outs += [o.reshape(shape), lse.reshape(shape)]

        of, ob = _hgrn(lbl2, qb, zf, zb, ib, batch, seq, l, rows=min(512, seq))

        oc = _diff_attn(diff_lambda[l].astype(F32), diff_norm[l].astype(F32).reshape(2 * HEAD_DIM, 1),
                        qct, kc, vct, batch, seq, l, tq=min(256, seq), kc=min(512, seq))

        hg = jnp.tile(hgrn_norm[l].astype(F32), N_HEADS).reshape(1, B_WIDTH)
        h2 = _out_proj(x2, a_outs, of, ob, gb, oc, hg, w_out[l].astype(BF16),
                       norm_post_mix[l].reshape(1, d), seq, tm)

        x2 = _ffn(h2, norm_pre_ffn[l].reshape(1, d), w_up[l].astype(BF16), conv_w[l],
                  conv_b[l].reshape(1, -1), w_down[l].astype(BF16), norm_post_ffn[l].reshape(1, d),
                  seq, tm, cn=256)
    return x2.reshape(batch, seq, d)
```

```python
import functools
import math

import jax
import jax.numpy as jnp
from jax import lax
from jax.experimental import pallas as pl
from jax.experimental.pallas import tpu as pltpu

F32 = jnp.float32
BF16 = jnp.bfloat16

HEAD_DIM = 64
N_HEADS = 4
A_WIDTH = N_HEADS * HEAD_DIM
B_WIDTH = N_HEADS * HEAD_DIM
C_WIDTH = N_HEADS * 2 * HEAD_DIM
ROPE_THETA = 500000.0
ROPE_DIM = HEAD_DIM // 4
ROPE_HALF = ROPE_DIM // 2
DILATIONS = (1, 4, 16)
BAND_HALF = 64
HGRN_CHUNK = 64
CONV_WIDTH = 3
EPS = 1e-6
NEG_INF = -1e30
LOG2E = math.log2(math.e)
LN2 = math.log(2.0)
EXP_CLAMP = 80.0

LANES = 128
BF16_ROWS = 16
VMEM_LIMIT = 56 * 1024 * 1024

OFF_QA, OFF_KA, OFF_VA = 0, 256, 512
OFF_QB, OFF_ZF, OFF_ZB, OFF_IB, OFF_GB = 768, 1024, 1280, 1536, 1792
OFF_QC, OFF_KC, OFF_VC = 2048, 2560, 3072


def _cparams(sem, flags=None):
    return pltpu.CompilerParams(dimension_semantics=sem, vmem_limit_bytes=VMEM_LIMIT, flags=flags)


def _rms(x, g):
    ms = jnp.mean(x * x, axis=-1, keepdims=True)
    return x * lax.rsqrt(ms + EPS) * g


def _dot(a, b):
    return jnp.dot(a, b, preferred_element_type=F32)


def _dot_nt(a, b):
    return lax.dot_general(a, b, (((1,), (1,)), ((), ())), preferred_element_type=F32)


def _dot_tn(a, b):
    return lax.dot_general(a, b, (((0,), (0,)), ((), ())), preferred_element_type=F32)


def _in_proj_kernel(x_ref, g_ref, w_ref, rc_ref, rs1_ref, rs2_ref,
                    qa_ref, ka_ref, va_ref, qa4_ref, ka4_ref, va4_ref, qa16_ref, ka16_ref, va16_ref,
                    qb_ref, zf_ref, zb_ref, ib_ref, gb_ref, qct_ref, kc_ref, vct_ref,
                    sq_ref, sk_ref, sv_ref):
    tm = x_ref.shape[0]
    xn = _rms(x_ref[...], g_ref[...]).astype(BF16)

    def emit_a(val, stage_ref, nat_ref, sub_refs):
        nat_ref[...] = val.astype(BF16)
        halves = range(A_WIDTH // LANES)
        for h in halves:
            stage_ref[h] = val[:, h * LANES:(h + 1) * LANES]
        for dil, ref in zip(DILATIONS[1:], sub_refs):
            for r in range(dil):
                ref[0, r, :, :] = jnp.concatenate(
                    [stage_ref[h, pl.ds(r, tm // dil, stride=dil), :] for h in halves], axis=1).astype(BF16)

    rc, rs1, rs2 = rc_ref[...], rs1_ref[...], rs2_ref[...]

    def proj(c0, n):
        return _dot(xn, w_ref[:, c0:c0 + n])

    def rope(a):
        outs = []
        for j in range(a.shape[1] // LANES):
            blk = a[:, j * LANES:(j + 1) * LANES]
            outs.append(blk * rc + pltpu.roll(blk, LANES - ROPE_HALF, 1) * rs1
                        + pltpu.roll(blk, ROPE_HALF, 1) * rs2)
        return outs

    def cat(blks):
        return jnp.concatenate(blks, axis=1)

    scale = 1.0 / math.sqrt(HEAD_DIM)
    emit_a(cat(rope(proj(OFF_QA, 256))) * (scale * LOG2E), sq_ref, qa_ref, (qa4_ref, qa16_ref))
    emit_a(cat(rope(proj(OFF_KA, 256))), sk_ref, ka_ref, (ka4_ref, ka16_ref))
    emit_a(proj(OFF_VA, 256), sv_ref, va_ref, (va4_ref, va16_ref))
    qb_ref[...] = proj(OFF_QB, 256).astype(BF16)
    zf_ref[...] = proj(OFF_ZF, 256)
    zb_ref[...] = proj(OFF_ZB, 256)
    ib_ref[...] = proj(OFF_IB, 256).astype(BF16)
    gb_ref[...] = proj(OFF_GB, 256)
    for half in range(2):
        qblks = rope(proj(OFF_QC + 256 * half, 256))
        kblks = rope(proj(OFF_KC + 256 * half, 256))
        v = proj(OFF_VC + 256 * half, 256)
        kc_ref[:, 256 * half:256 * (half + 1)] = cat(kblks).astype(BF16)
        for j in range(2):
            r0 = 256 * half + LANES * j
            qct_ref[0, r0:r0 + LANES, :] = (qblks[j] * (scale * LOG2E)).T.astype(BF16)
            vct_ref[0, r0:r0 + LANES, :] = v[:, j * LANES:(j + 1) * LANES].T.astype(BF16)


def _in_proj(x2, g, w, rc, rs1, rs2, batch, seq, tm):
    t, d = x2.shape
    n_cols = w.shape[1]
    tiles_per_seq = seq // tm
    row = lambda i: (i, 0)
    const = lambda i: (0, 0)
    ropei = lambda i: (i % tiles_per_seq, 0)
    tr = lambda i: (i // tiles_per_seq, 0, i % tiles_per_seq)
    nat = lambda n, dt: jax.ShapeDtypeStruct((t, n), dt)
    sub_shape = lambda dil: jax.ShapeDtypeStruct((batch, dil, seq // dil, A_WIDTH), BF16)
    sub_spec = lambda dil: pl.BlockSpec((1, dil, tm // dil, A_WIDTH),
                                        lambda i: (i // tiles_per_seq, 0, i % tiles_per_seq, 0))
    out_shape = ([nat(256, BF16)] * 3
                 + [sub_shape(DILATIONS[1])] * 3 + [sub_shape(DILATIONS[2])] * 3
                 + [nat(256, BF16), nat(256, F32), nat(256, F32), nat(256, BF16), nat(256, F32)]
                 + [jax.ShapeDtypeStruct((batch, C_WIDTH, seq), BF16),
                    nat(C_WIDTH, BF16),
                    jax.ShapeDtypeStruct((batch, C_WIDTH, seq), BF16)])
    out_specs = ([pl.BlockSpec((tm, 256), row)] * 3 + [sub_spec(DILATIONS[1])] * 3 + [sub_spec(DILATIONS[2])] * 3
                 + [pl.BlockSpec((tm, 256), row)] * 5
                 + [pl.BlockSpec((1, C_WIDTH, tm), tr), pl.BlockSpec((tm, C_WIDTH), row),
                    pl.BlockSpec((1, C_WIDTH, tm), tr)])
    return pl.pallas_call(
        _in_proj_kernel,
        grid=(t // tm,),
        in_specs=[pl.BlockSpec((tm, d), row), pl.BlockSpec((1, d), const),
                  pl.BlockSpec((d, n_cols), const),
                  pl.BlockSpec((tm, LANES), ropei), pl.BlockSpec((tm, LANES), ropei),
                  pl.BlockSpec((tm, LANES), ropei)],
        out_specs=out_specs,
        out_shape=out_shape,
        scratch_shapes=[pltpu.VMEM((A_WIDTH // LANES, tm, LANES), F32)] * 3,
        compiler_params=_cparams(("parallel",)),
        name="in_proj",
    )(x2, g, w, rc, rs1, rs2)


def _banded_kernel(q_ref, k_ref, v_ref, o_ref, lse_ref, *, seq_len, q_block, sub, k_win):
    i = pl.program_id(1)
    rows = N_HEADS * sub
    lane = lax.broadcasted_iota(jnp.int32, (rows, A_WIDTH), 1)
    row_head = lax.broadcasted_iota(jnp.int32, (rows, A_WIDTH), 0) // sub
    own_lanes = (lane // HEAD_DIM) == row_head
    rel = (lax.broadcasted_iota(jnp.int32, (rows, k_win), 1)
           - lax.broadcasted_iota(jnp.int32, (rows, k_win), 0) % sub)
    head_lane = lax.broadcasted_iota(jnp.int32, (sub, A_WIDTH), 1) // HEAD_DIM
    n_sub = q_block // sub
    q0s = [i * q_block + sb * sub for sb in range(n_sub)]
    kss = [pl.multiple_of(jnp.clip(q0 - BAND_HALF, 0, seq_len - k_win), BAND_HALF) for q0 in q0s]

    def score(sb):
        qblk = q_ref[0, sb * sub:(sb + 1) * sub, :]
        q4 = jnp.concatenate([qblk] * N_HEADS, axis=0)
        q4 = jnp.where(own_lanes, q4, jnp.zeros_like(q4))
        return _dot_nt(q4, k_ref[0, pl.ds(kss[sb], k_win), :])

    def softmax(sb, s):
        band = jnp.abs(rel + (kss[sb] - q0s[sb])) <= BAND_HALF
        s = jnp.where(band, s, NEG_INF)
        m = jnp.max(s, axis=-1, keepdims=True)
        p = jnp.exp2(s - m)
        den = jnp.sum(p, axis=-1, keepdims=True)
        return p.astype(BF16), den, (m + jnp.log2(den)) * LN2

    def per_head(x4):
        out = x4[(N_HEADS - 1) * sub:N_HEADS * sub]
        for h in reversed(range(N_HEADS - 1)):
            out = jnp.where(head_lane == h, x4[h * sub:(h + 1) * sub], out)
        return out

    ss = [score(sb) for sb in range(n_sub)]
    sm = [softmax(sb, s) for sb, s in enumerate(ss)]
    pv = [_dot(p, v_ref[0, pl.ds(kss[sb], k_win), :]) for sb, (p, _, _) in enumerate(sm)]
    for sb in range(n_sub):
        _, den, lse = sm[sb]
        o_ref[0, sb * sub:(sb + 1) * sub, :] = per_head(pv[sb]) / per_head(den)
        lse_ref[0, sb * sub:(sb + 1) * sub, :] = per_head(lse)


def _banded(q, k, v):
    n_seq, seq_len, w = q.shape
    q_block = min(512, seq_len)
    sub = min(128, seq_len)
    k_win = min(sub + 2 * BAND_HALF, seq_len)
    kern = functools.partial(_banded_kernel, seq_len=seq_len, q_block=q_block, sub=sub, k_win=k_win)
    full = pl.BlockSpec((1, seq_len, w), lambda s, i: (s, 0, 0))
    blk = pl.BlockSpec((1, q_block, w), lambda s, i: (s, i, 0))
    return pl.pallas_call(
        kern,
        grid=(n_seq, seq_len // q_block),
        in_specs=[blk, full, full],
        out_specs=[blk, blk],
        out_shape=[jax.ShapeDtypeStruct((n_seq, seq_len, w), F32)] * 2,
        compiler_params=_cparams(("parallel", "arbitrary")),
        name="banded_attn",
    )(q, k, v)


def _split3(x):
    hi = x.astype(BF16)
    r1 = x - hi.astype(F32)
    mid = r1.astype(BF16)
    lo = (r1 - mid.astype(F32)).astype(BF16)
    return hi, mid, lo


def _hgrn_kernel(lbl_ref, lbd_ref, qf_ref, zf_ref, vf_ref, qb_ref, zb_ref, vb_ref,
                 of_ref, ob_ref, sf_ref, sb_ref, *, depth, layer, n_chunks):
    c = HGRN_CHUNK
    rows = n_chunks * c
    group = lbd_ref.shape[0]

    @pl.when(pl.program_id(2) == 0)
    def _():
        sf_ref[...] = jnp.zeros_like(sf_ref)
        sb_ref[...] = jnp.zeros_like(sb_ref)

    def lower_bound(logits):
        e = jnp.exp(logits - jnp.max(logits, axis=0, keepdims=True))
        p = e / jnp.sum(e, axis=0, keepdims=True)
        lb = jnp.zeros((1, LANES), F32)
        for j in range(1, layer + 1):
            lb = lb + p[j:j + 1, :]
        return lb

    lb_f = lower_bound(lbl_ref[0:depth, :])
    lb_b = lower_bound(lbl_ref[depth:2 * depth, :])

    ti = lax.broadcasted_iota(jnp.int32, (c, c), 0)
    si = lax.broadcasted_iota(jnp.int32, (c, c), 1)
    tril = si <= ti
    triu = si >= ti
    lane_c = lax.broadcasted_iota(jnp.int32, (c, LANES), 1) < HEAD_DIM
    lane_b = lax.broadcasted_iota(jnp.int32, (rows, LANES), 1) < HEAD_DIM
    er = lax.broadcasted_iota(jnp.int32, (LANES, LANES), 0) < HEAD_DIM
    ec = lax.broadcasted_iota(jnp.int32, (LANES, LANES), 1) < HEAD_DIM
    same_head = er == ec
    lbd = lbd_ref[...]

    def cumsum(logf):
        parts = jnp.concatenate(_split3(logf), axis=1)
        outs = []
        for g in range(rows // group):
            r = _dot(lbd, parts[g * group:(g + 1) * group, :])
            outs.append(r[:, 0:LANES] + r[:, LANES:2 * LANES] + r[:, 2 * LANES:3 * LANES])
        return jnp.concatenate(outs, axis=0)

    def chunk_rows(x, off):
        return jnp.concatenate(
            [jnp.broadcast_to(x[j * c + off:j * c + off + 1, :], (c, LANES)) for j in range(n_chunks)], axis=0)

    rs = [slice(j * c, (j + 1) * c) for j in range(n_chunks)]

    def prepare(q_ref, z_ref, v_ref, lb, forward):
        q = q_ref[...].astype(F32)
        z = z_ref[...]
        v = v_ref[...]
        logf = jnp.log(lb + (1.0 - lb) * jax.nn.sigmoid(z))
        kk = (1.0 - lb) * jax.nn.sigmoid(-z)
        a = cumsum(logf)
        last = chunk_rows(a, c - 1)
        if forward:
            e = a
            mid = chunk_rows(a, c // 2 - 1)
            q_in = q * jnp.exp(a)
            k_st = kk * jnp.exp(last - a)
            tri = tril
        else:
            e = a - logf
            mid = chunk_rows(e, c // 2)
            q_in = q * jnp.exp(last - e)
            k_st = kk * jnp.exp(e)
            tri = triu
        sgn = 1.0 if forward else -1.0
        qd = q * jnp.exp(jnp.minimum(sgn * (e - mid), EXP_CLAMP))
        kd = (kk * jnp.exp(jnp.minimum(sgn * (mid - e), EXP_CLAMP))).astype(BF16)
        q0 = jnp.where(lane_b, qd, 0.0).astype(BF16)
        q1 = jnp.where(lane_b, 0.0, qd).astype(BF16)
        decay = [jnp.exp(a[(j + 1) * c - 1:(j + 1) * c, :]) for j in range(n_chunks)]
        return dict(q0=q0, q1=q1, kd=kd, v=v, q_in=q_in.astype(BF16), k_st=k_st.astype(BF16), tri=tri,
                    decay=decay, forward=forward)

    def first_dots(d):
        return ([_dot_nt(d["q0"][r], d["kd"][r]) for r in rs], [_dot_nt(d["q1"][r], d["kd"][r]) for r in rs],
                [_dot_tn(d["v"][r], d["k_st"][r]) for r in rs])

    def intra_dots(d, s0, s1):
        s0 = [jnp.where(d["tri"], s, 0.0).astype(BF16) for s in s0]
        s1 = [jnp.where(d["tri"], s, 0.0).astype(BF16) for s in s1]
        return [jnp.where(lane_c, _dot(x0, d["v"][r]), _dot(x1, d["v"][r])) for x0, x1, r in zip(s0, s1, rs)]

    def scan(d, ut, st_ref):
        st = st_ref[...]
        states = [None] * n_chunks
        for j in (range(n_chunks) if d["forward"] else reversed(range(n_chunks))):
            states[j] = st.astype(BF16)
            st = st * d["decay"][j] + jnp.where(same_head, ut[j], 0.0)
        st_ref[...] = st
        return states

    def finish(d, states, intra, o_ref):
        for j, r in enumerate(rs):
            o_ref[r, :] = _dot_nt(d["q_in"][r], states[j]) + intra[j]

    df = prepare(qf_ref, zf_ref, vf_ref, lb_f, True)
    db = prepare(qb_ref, zb_ref, vb_ref, lb_b, False)
    f0, f1, fu = first_dots(df)
    b0, b1, bu = first_dots(db)
    fi = intra_dots(df, f0, f1)
    bi = intra_dots(db, b0, b1)
    fs = scan(df, fu, sf_ref)
    bs = scan(db, bu, sb_ref)
    finish(df, fs, fi, of_ref)
    finish(db, bs, bi, ob_ref)


def _hgrn(lb_logits2, q, zf, zb, v, batch, seq, layer, rows):
    t = q.shape[0]
    depth = lb_logits2.shape[0] // 2
    n = seq // rows
    fwd = lambda b, p, i: (b * n + i, p)
    bwd = lambda b, p, i: (b * n + (n - 1 - i), p)
    kern = functools.partial(_hgrn_kernel, depth=depth, layer=layer, n_chunks=rows // HGRN_CHUNK)
    spec = lambda im: pl.BlockSpec((rows, LANES), im)
    group = min(256, rows)
    idx = jnp.arange(group)
    lbd = ((idx[None, :] <= idx[:, None])
           & (idx[None, :] // HGRN_CHUNK == idx[:, None] // HGRN_CHUNK)).astype(BF16)
    return pl.pallas_call(
        kern,
        grid=(batch, B_WIDTH // LANES, n),
        in_specs=[pl.BlockSpec((2 * depth, LANES), lambda b, p, i: (0, p)),
                  pl.BlockSpec((group, group), lambda b, p, i: (0, 0)),
                  spec(fwd), spec(fwd), spec(fwd), spec(bwd), spec(bwd), spec(bwd)],
        out_specs=[spec(fwd), spec(bwd)],
        out_shape=[jax.ShapeDtypeStruct((t, B_WIDTH), F32)] * 2,
        scratch_shapes=[pltpu.VMEM((LANES, LANES), F32), pltpu.VMEM((LANES, LANES), F32)],
        compiler_params=_cparams(("parallel", "parallel", "arbitrary")),
        name="hgrn2",
    )(lb_logits2, lbd, q, zf, v, q, zb, v)


def _diff_kernel(lam_ref, g_ref, q0_ref, q1_ref, q2_ref, k_ref, vt_ref, o_ref,
                 s1_ref, s2_ref, p1_ref, p2_ref, st_ref, *, seq, tq, kc, unroll, lam_init):
    lp = lam_ref[...]
    lam = (jnp.exp(jnp.sum(lp[0:1, :] * lp[1:2, :], axis=1, keepdims=True))
           - jnp.exp(jnp.sum(lp[2:3, :] * lp[3:4, :], axis=1, keepdims=True)) + lam_init)
    n = seq // kc
    sub = 8

    def split(qt):
        row = lax.broadcasted_iota(jnp.int32, qt.shape, 0)
        zero = jnp.zeros_like(qt)
        return jnp.where(row < HEAD_DIM, qt, zero), jnp.where(row < HEAD_DIM, zero, qt)

    def scores(k0, q, s_ref, mrun):
        s = _dot(k_ref[pl.ds(k0, kc), :], q)
        s_ref[pl.ds(k0, kc), :] = s
        return jnp.maximum(mrun, jnp.max(s.reshape(kc // sub, sub, tq), axis=0))

    def expo(k0, s_ref, p_ref, m, lrun):
        p = jnp.exp2(s_ref[pl.ds(k0, kc), :] - m)
        p_ref[pl.ds(k0, kc), :] = p.astype(BF16)
        return lrun + jnp.sum(p.reshape(kc // sub, sub, tq), axis=0)

    part = lambda val: jnp.full((sub, tq), val, F32)
    loop = functools.partial(lax.fori_loop, 0, n, unroll=unroll)
    fill_loop = functools.partial(lax.fori_loop, 0, n, unroll=min(4, n))
    rows = lambda a: slice(a * sub, (a + 1) * sub)
    colmax = lambda a: jnp.max(st_ref[rows(a), :], axis=0, keepdims=True)
    colsum = lambda a: jnp.sum(st_ref[rows(a), :], axis=0, keepdims=True)

    @pl.when(pl.program_id(2) == 0)
    def _():
        qa1, qa2 = split(q0_ref[0])

        def fill_s(j, carry):
            k0 = pl.multiple_of(j * kc, kc)
            return scores(k0, qa1, s1_ref, carry[0]), scores(k0, qa2, s2_ref, carry[1])

        mr1, mr2 = fill_loop(fill_s, (part(NEG_INF), part(NEG_INF)))
        ma1 = jnp.max(mr1, axis=0, keepdims=True)
        ma2 = jnp.max(mr2, axis=0, keepdims=True)
        qb1, qb2 = split(q1_ref[0])

        def fill_e(j, carry):
            l1, l2, mr1, mr2 = carry
            k0 = pl.multiple_of(j * kc, kc)
            l1 = expo(k0, s1_ref, p1_ref, ma1, l1)
            mr1 = scores(k0, qb1, s1_ref, mr1)
            l2 = expo(k0, s2_ref, p2_ref, ma2, l2)
            mr2 = scores(k0, qb2, s2_ref, mr2)
            return l1, l2, mr1, mr2

        l1, l2, mr1, mr2 = fill_loop(fill_e, (part(0.0), part(0.0), part(NEG_INF), part(NEG_INF)))
        st_ref[rows(0), :] = mr1
        st_ref[rows(1), :] = mr2
        st_ref[rows(2), :] = l1
        st_ref[rows(3), :] = l2

    m1, m2 = colmax(0), colmax(1)
    l1, l2 = colsum(2), colsum(3)
    c16 = jnp.broadcast_to((lam * l1 / l2).astype(BF16), (BF16_ROWS, tq))
    qs1, qs2 = split(q2_ref[0])

    def body(j, carry):
        acc, e1, e2, mr1, mr2 = carry
        k0 = pl.multiple_of(j * kc, kc)
        p2 = p2_ref[pl.ds(k0, kc), :].reshape(kc // BF16_ROWS, BF16_ROWS, tq)
        w = p1_ref[pl.ds(k0, kc), :] - (p2 * c16).reshape(kc, tq)
        acc = acc + _dot(vt_ref[0, :, pl.ds(k0, kc)], w)
        e1 = expo(k0, s1_ref, p1_ref, m1, e1)
        mr1 = scores(k0, qs1, s1_ref, mr1)
        e2 = expo(k0, s2_ref, p2_ref, m2, e2)
        mr2 = scores(k0, qs2, s2_ref, mr2)
        return acc, e1, e2, mr1, mr2

    acc, e1, e2, mr1, mr2 = loop(body, (jnp.zeros((2 * HEAD_DIM, tq), F32), part(0.0), part(0.0),
                                        part(NEG_INF), part(NEG_INF)))
    st_ref[rows(0), :] = mr1
    st_ref[rows(1), :] = mr2
    st_ref[rows(2), :] = e1
    st_ref[rows(3), :] = e2
    o = acc / l1
    ms = jnp.mean(o * o, axis=0, keepdims=True)
    y = o * lax.rsqrt(ms + EPS) * g_ref[...] * (1.0 - lam_init)
    o_ref[...] = y.T.astype(o_ref.dtype)


def _diff_attn(lam_p, g_col, qct, kc_nat, vct, batch, seq, layer, tq, kc):
    t = kc_nat.shape[0]
    lam_init = 0.8 - 0.6 * math.exp(-0.3 * layer)
    nq = seq // tq
    kern = functools.partial(_diff_kernel, seq=seq, tq=tq, kc=kc, unroll=min(16, seq // kc), lam_init=lam_init)
    q_spec = lambda ahead: pl.BlockSpec((1, 2 * HEAD_DIM, tq),
                                        lambda b, h, i: (b, h, jnp.minimum(i + ahead, nq - 1)))
    return pl.pallas_call(
        kern,
        grid=(batch, N_HEADS, nq),
        in_specs=[pl.BlockSpec(lam_p.shape, lambda b, h, i: (0, 0)),
                  pl.BlockSpec((2 * HEAD_DIM, 1), lambda b, h, i: (0, 0)),
                  q_spec(0), q_spec(1), q_spec(2),
                  pl.BlockSpec((seq, 2 * HEAD_DIM), lambda b, h, i: (b, h)),
                  pl.BlockSpec((1, 2 * HEAD_DIM, seq), lambda b, h, i: (b, h, 0))],
        out_specs=pl.BlockSpec((tq, 2 * HEAD_DIM), lambda b, h, i: (b * nq + i, h)),
        out_shape=jax.ShapeDtypeStruct((t, C_WIDTH), BF16),
        scratch_shapes=[pltpu.VMEM((seq, tq), F32), pltpu.VMEM((seq, tq), F32),
                        pltpu.VMEM((seq, tq), BF16), pltpu.VMEM((seq, tq), BF16),
                        pltpu.VMEM((32, tq), F32)],
        compiler_params=_cparams(("parallel", "parallel", "arbitrary")),
        name="diff_attn",
    )(lam_p, g_col, qct, qct, qct, kc_nat, vct)


def _out_proj_kernel(x_ref, o1_ref, l1_ref, o2_ref, l2_ref, o3_ref, l3_ref,
                     of_ref, ob_ref, gb_ref, oc_ref, hg_ref, w_ref, g_ref, h_ref, *stage_refs):
    tm = x_ref.shape[0]
    ab_width = A_WIDTH + B_WIDTH
    mix_c = _dot(oc_ref[...], w_ref[ab_width:, :])

    def natural(ref, dil, stage_ref):
        halves = range(A_WIDTH // LANES)
        for r in range(dil):
            blk = ref[0, r]
            for h in halves:
                stage_ref[h, pl.ds(r, tm // dil, stride=dil), :] = blk[:, h * LANES:(h + 1) * LANES]
        return jnp.concatenate([stage_ref[h] for h in halves], axis=1)

    o2, l2 = natural(o2_ref, DILATIONS[1], stage_refs[0]), natural(l2_ref, DILATIONS[1], stage_refs[1])
    o3, l3 = natural(o3_ref, DILATIONS[2], stage_refs[2]), natural(l3_ref, DILATIONS[2], stage_refs[3])
    l1 = l1_ref[...]
    mx = jnp.maximum(jnp.maximum(l1, l2), l3)
    w1, w2, w3 = jnp.exp(l1 - mx), jnp.exp(l2 - mx), jnp.exp(l3 - mx)
    oa = (w1 * o1_ref[...] + w2 * o2 + w3 * o3) / (w1 + w2 + w3)

    y = of_ref[...] + ob_ref[...]
    y2 = y * y
    lane = lax.broadcasted_iota(jnp.int32, y.shape, 1)
    ms = jnp.zeros_like(y)
    for h in range(N_HEADS):
        in_head = (lane >= h * HEAD_DIM) & (lane < (h + 1) * HEAD_DIM)
        ms_h = jnp.sum(jnp.where(in_head, y2, 0.0), axis=-1, keepdims=True) * (1.0 / HEAD_DIM)
        ms = jnp.where(in_head, ms_h, ms)
    gb = gb_ref[...]
    ob = y * lax.rsqrt(ms + EPS) * hg_ref[...] * (gb * jax.nn.sigmoid(gb))

    cat = jnp.concatenate([oa.astype(BF16), ob.astype(BF16)], axis=1)
    mix = _dot(cat, w_ref[0:ab_width, :]) + mix_c
    h_ref[...] = x_ref[...] + _rms(mix, g_ref[...])


def _out_proj(x2, a_outs, of, ob, gb, oc, hg, w, g, seq, tm):
    t, d = x2.shape
    tiles_per_seq = seq // tm
    row = lambda i: (i, 0)
    const = lambda i: (0, 0)
    r256 = pl.BlockSpec((tm, 256), row)
    sub = lambda dil: pl.BlockSpec((1, dil, tm // dil, A_WIDTH),
                                   lambda i: (i // tiles_per_seq, 0, i % tiles_per_seq, 0))
    a_specs = [r256, r256] + [sub(DILATIONS[1])] * 2 + [sub(DILATIONS[2])] * 2
    return pl.pallas_call(
        _out_proj_kernel,
        grid=(t // tm,),
        in_specs=[pl.BlockSpec((tm, d), row)] + a_specs + [r256] * 3 + [
            pl.BlockSpec((tm, C_WIDTH), row), pl.BlockSpec((1, B_WIDTH), const),
            pl.BlockSpec(w.shape, const), pl.BlockSpec((1, d), const)],
        out_specs=pl.BlockSpec((tm, d), row),
        out_shape=jax.ShapeDtypeStruct((t, d), F32),
        scratch_shapes=[pltpu.VMEM((A_WIDTH // LANES, tm, LANES), F32)] * 4,
        compiler_params=_cparams(("parallel",)),
        name="out_proj",
    )(x2, *a_outs, of, ob, gb, oc, hg, w, g)


def _ffn_kernel(hp_ref, h_ref, hn_ref, gpre_ref, wup_ref, cw_ref, cb_ref, wdn_ref, gpost_ref,
                o_ref, xe_ref, act_ref, *, tm, tiles_per_seq, d_ff, cn):
    i = pl.program_id(0)
    halo = BF16_ROWS
    first = (i % tiles_per_seq) == 0
    last = (i % tiles_per_seq) == tiles_per_seq - 1
    gpre = gpre_ref[...]
    h = h_ref[...]
    xe_ref[halo:halo + tm, :] = _rms(h, gpre).astype(BF16)
    xe_ref[0:halo, :] = jnp.where(first, 0.0, _rms(hp_ref[...], gpre)).astype(BF16)
    xe_ref[halo + tm:2 * halo + tm, :] = jnp.where(last, 0.0, _rms(hn_ref[...], gpre)).astype(BF16)
    xe = xe_ref[...]

    def conv(c0):
        u = _dot(xe, wup_ref[:, c0:c0 + cn])
        cw = cw_ref[:, c0:c0 + cn]
        return (u[halo - 1:halo - 1 + tm] * cw[0:1] + u[halo:halo + tm] * cw[1:2]
                + u[halo + 1:halo + 1 + tm] * cw[2:3] + cb_ref[:, c0:c0 + cn])

    for c in range(d_ff // cn):
        gate = conv(c * cn)
        val = conv(d_ff + c * cn)
        act_ref[:, c * cn:(c + 1) * cn] = (gate * jax.nn.sigmoid(gate) * val).astype(BF16)
    ff = _dot(act_ref[...], wdn_ref[...])
    o_ref[...] = h + _rms(ff, gpost_ref[...])


def _ffn(h2, gpre, wup, cw, cb, wdn, gpost, seq, tm, cn):
    t, d = h2.shape
    d_ff = wdn.shape[0]
    halo = BF16_ROWS
    tiles_per_seq = seq // tm
    hb = tm // halo
    const = lambda i: (0, 0)
    kern = functools.partial(_ffn_kernel, tm=tm, tiles_per_seq=tiles_per_seq, d_ff=d_ff, cn=cn)
    return pl.pallas_call(
        kern,
        grid=(t // tm,),
        in_specs=[pl.BlockSpec((halo, d), lambda i: (jnp.maximum(i * hb - 1, 0), 0)),
                  pl.BlockSpec((tm, d), lambda i: (i, 0)),
                  pl.BlockSpec((halo, d), lambda i: (jnp.minimum((i + 1) * hb, t // halo - 1), 0)),
                  pl.BlockSpec((1, d), const),
                  pl.BlockSpec(wup.shape, const), pl.BlockSpec(cw.shape, const),
                  pl.BlockSpec(cb.shape, const), pl.BlockSpec(wdn.shape, const),
                  pl.BlockSpec((1, d), const)],
        out_specs=pl.BlockSpec((tm, d), lambda i: (i, 0)),
        out_shape=jax.ShapeDtypeStruct((t, d), F32),
        scratch_shapes=[pltpu.VMEM((tm + 2 * halo, d), BF16), pltpu.VMEM((tm, d_ff), BF16)],
        compiler_params=_cparams(("parallel",)),
        name="conv_ffn",
    )(h2, h2, h2, gpre, wup, cw, cb, wdn, gpost)


def _rope_tables(seq):
    pos = jnp.arange(seq, dtype=F32)
    inv = ROPE_THETA ** (-jnp.arange(0, ROPE_DIM, 2, dtype=F32) / ROPE_DIM)
    ang = pos[:, None] * inv[None, :]
    cos, sin = jnp.cos(ang), jnp.sin(ang)
    rest = HEAD_DIM - ROPE_DIM
    one, zero = jnp.ones((seq, rest), F32), jnp.zeros((seq, rest), F32)
    zh = jnp.zeros((seq, ROPE_HALF), F32)
    rep = LANES // HEAD_DIM
    rc = jnp.tile(jnp.concatenate([cos, cos, one], axis=1), (1, rep))
    rs1 = jnp.tile(jnp.concatenate([-sin, zh, zero], axis=1), (1, rep))
    rs2 = jnp.tile(jnp.concatenate([zh, sin, zero], axis=1), (1, rep))
    return rc, rs1, rs2


def kernel(x, w_in, w_out, lb_logits, hgrn_norm, diff_lambda, diff_norm, w_up, conv_w, conv_b,
           w_down, norm_pre_mix, norm_post_mix, norm_pre_ffn, norm_post_ffn):
    batch, seq, d = x.shape
    depth = w_in.shape[0]
    tm = min(512, seq)
    rc, rs1, rs2 = _rope_tables(seq)
    lbl2 = lb_logits.astype(F32).reshape(2 * depth, B_WIDTH)
    x2 = x.reshape(batch * seq, d)
    for l in range(depth):
        (qa, ka, va, qa4, ka4, va4, qa16, ka16, va16, qb, zf, zb, ib, gb, qct, kc, vct) = _in_proj(
            x2, norm_pre_mix[l].reshape(1, d), w_in[l].astype(BF16), rc, rs1, rs2, batch, seq, tm)

        a_outs = []
        for dil, qkv in zip(DILATIONS, ((qa, ka, va), (qa4, ka4, va4), (qa16, ka16, va16))):
            o, lse = _banded(*(a.reshape(batch * dil, seq // dil, A_WIDTH) for a in qkv))
            shape = (batch * seq, A_WIDTH) if dil == 1 else (batch, dil, seq // dil, A_WIDTH)
            a_outs += [o.reshape(shape), lse.reshape(shape)]

        of, ob = _hgrn(lbl2, qb, zf, zb, ib, batch, seq, l, rows=min(512, seq))

        oc = _diff_attn(diff_lambda[l].astype(F32), diff_norm[l].astype(F32).reshape(2 * HEAD_DIM, 1),
                        qct, kc, vct, batch, seq, l, tq=min(256, seq), kc=min(512, seq))

        hg = jnp.tile(hgrn_norm[l].astype(F32), N_HEADS).reshape(1, B_WIDTH)
        h2 = _out_proj(x2, a_outs, of, ob, gb, oc, hg, w_out[l].astype(BF16),
                       norm_post_mix[l].reshape(1, d), seq, tm)

        x2 = _ffn(h2, norm_pre_ffn[l].reshape(1, d), w_up[l].astype(BF16), conv_w[l],
                  conv_b[l].reshape(1, -1), w_down[l].astype(BF16), norm_post_ffn[l].reshape(1, d),
                  seq, tm, cn=256)
    return x2.reshape(batch, seq, d)
```

```python
import functools
import math
from typing import NamedTuple

import jax
import jax.numpy as jnp
from jax import lax
from jax.experimental import pallas as pl
from jax.experimental.pallas import tpu as pltpu

F32 = jnp.float32
BF16 = jnp.bfloat16

HEAD_DIM = 64
N_HEADS = 4
A_WIDTH = N_HEADS * HEAD_DIM
B_WIDTH = N_HEADS * HEAD_DIM
C_WIDTH = N_HEADS * 2 * HEAD_DIM
ROPE_THETA = 500000.0
ROPE_DIM = HEAD_DIM // 4
ROPE_HALF = ROPE_DIM // 2
DILATIONS = (1, 4, 16)
BAND_HALF = 64
HGRN_CHUNK = 64
CONV_WIDTH = 3
EPS = 1e-6
NEG_INF = -1e30
LOG2E = math.log2(math.e)
LN2 = math.log(2.0)
EXP_CLAMP = 80.0

LANES = 128
BF16_ROWS = 16
MXU_TILE = 256
VMEM_LIMIT = 56 * 1024 * 1024

OFF_QA, OFF_KA, OFF_VA = 0, 256, 512
OFF_QB, OFF_ZF, OFF_ZB, OFF_IB, OFF_GB = 768, 1024, 1280, 1536, 1792
OFF_QC, OFF_KC, OFF_VC = 2048, 2560, 3072


def _cparams(sem):
    return pltpu.CompilerParams(dimension_semantics=sem, vmem_limit_bytes=VMEM_LIMIT)


def _rms(x, g):
    ms = jnp.mean(x * x, axis=-1, keepdims=True)
    return x * lax.rsqrt(ms + EPS) * g


def _dot(a, b):
    return jnp.dot(a, b, preferred_element_type=F32)


def _dot_nt(a, b):
    return lax.dot_general(a, b, (((1,), (1,)), ((), ())), preferred_element_type=F32)


def _dot_tn(a, b):
    return lax.dot_general(a, b, (((0,), (0,)), ((), ())), preferred_element_type=F32)


def _in_proj_kernel(x_ref, g_ref, w_ref, rc_ref, rs1_ref, rs2_ref,
                    qa_ref, ka_ref, va_ref, qa4_ref, ka4_ref, va4_ref, qa16_ref, ka16_ref, va16_ref,
                    qb_ref, zf_ref, zb_ref, ib_ref, gb_ref, qct_ref, kc_ref, vct_ref,
                    sq_ref, sk_ref, sv_ref):
    tm = x_ref.shape[0]
    xn = _rms(x_ref[...], g_ref[...]).astype(BF16)

    def emit_a(val, stage_ref, nat_ref, sub_refs):
        nat_ref[...] = val.astype(BF16)
        halves = range(A_WIDTH // LANES)
        for h in halves:
            stage_ref[h] = val[:, h * LANES:(h + 1) * LANES]
        for dil, ref in zip(DILATIONS[1:], sub_refs):
            for r in range(dil):
                ref[0, r, :, :] = jnp.concatenate(
                    [stage_ref[h, pl.ds(r, tm // dil, stride=dil), :] for h in halves], axis=1).astype(BF16)

    rc, rs1, rs2 = rc_ref[...], rs1_ref[...], rs2_ref[...]

    def proj(c0, n):
        return _dot(xn, w_ref[:, c0:c0 + n])

    def rope(a):
        outs = []
        for j in range(a.shape[1] // LANES):
            blk = a[:, j * LANES:(j + 1) * LANES]
            outs.append(blk * rc + pltpu.roll(blk, LANES - ROPE_HALF, 1) * rs1
                        + pltpu.roll(blk, ROPE_HALF, 1) * rs2)
        return outs

    def cat(blks):
        return jnp.concatenate(blks, axis=1)

    scale = 1.0 / math.sqrt(HEAD_DIM)
    for half in range(2):
        qblks = rope(proj(OFF_QC + 256 * half, 256))
        kblks = rope(proj(OFF_KC + 256 * half, 256))
        v = proj(OFF_VC + 256 * half, 256)
        kc_ref[:, 256 * half:256 * (half + 1)] = cat(kblks).astype(BF16)
        for j in range(2):
            r0 = 256 * half + LANES * j
            qct_ref[0, r0:r0 + LANES, :] = (qblks[j] * (scale * LOG2E)).T.astype(BF16)
            vct_ref[0, r0:r0 + LANES, :] = v[:, j * LANES:(j + 1) * LANES].T.astype(BF16)
    emit_a(cat(rope(proj(OFF_QA, 256))) * (scale * LOG2E), sq_ref, qa_ref, (qa4_ref, qa16_ref))
    emit_a(cat(rope(proj(OFF_KA, 256))), sk_ref, ka_ref, (ka4_ref, ka16_ref))
    emit_a(proj(OFF_VA, 256), sv_ref, va_ref, (va4_ref, va16_ref))
    qb_ref[...] = proj(OFF_QB, 256).astype(BF16)
    zf_ref[...] = proj(OFF_ZF, 256)
    zb_ref[...] = proj(OFF_ZB, 256)
    ib_ref[...] = proj(OFF_IB, 256).astype(BF16)
    gb_ref[...] = proj(OFF_GB, 256)


def _in_proj(x2, g, w, rc, rs1, rs2, batch, seq, tm):
    t, d = x2.shape
    n_cols = w.shape[1]
    tiles_per_seq = seq // tm
    row = lambda i: (i, 0)
    const = lambda i: (0, 0)
    ropei = lambda i: (i % tiles_per_seq, 0)
    tr = lambda i: (i // tiles_per_seq, 0, i % tiles_per_seq)
    nat = lambda n, dt: jax.ShapeDtypeStruct((t, n), dt)
    sub_shape = lambda dil: jax.ShapeDtypeStruct((batch, dil, seq // dil, A_WIDTH), BF16)
    sub_spec = lambda dil: pl.BlockSpec((1, dil, tm // dil, A_WIDTH),
                                        lambda i: (i // tiles_per_seq, 0, i % tiles_per_seq, 0))
    out_shape = ([nat(256, BF16)] * 3
                 + [sub_shape(DILATIONS[1])] * 3 + [sub_shape(DILATIONS[2])] * 3
                 + [nat(256, BF16), nat(256, F32), nat(256, F32), nat(256, BF16), nat(256, F32)]
                 + [jax.ShapeDtypeStruct((batch, C_WIDTH, seq), BF16),
                    nat(C_WIDTH, BF16),
                    jax.ShapeDtypeStruct((batch, C_WIDTH, seq), BF16)])
    out_specs = ([pl.BlockSpec((tm, 256), row)] * 3 + [sub_spec(DILATIONS[1])] * 3 + [sub_spec(DILATIONS[2])] * 3
                 + [pl.BlockSpec((tm, 256), row)] * 5
                 + [pl.BlockSpec((1, C_WIDTH, tm), tr), pl.BlockSpec((tm, C_WIDTH), row),
                    pl.BlockSpec((1, C_WIDTH, tm), tr)])
    return pl.pallas_call(
        _in_proj_kernel,
        grid=(t // tm,),
        in_specs=[pl.BlockSpec((tm, d), row), pl.BlockSpec((1, d), const),
                  pl.BlockSpec((d, n_cols), const),
                  pl.BlockSpec((tm, LANES), ropei), pl.BlockSpec((tm, LANES), ropei),
                  pl.BlockSpec((tm, LANES), ropei)],
        out_specs=out_specs,
        out_shape=out_shape,
        scratch_shapes=[pltpu.VMEM((A_WIDTH // LANES, tm, LANES), F32)] * 3,
        compiler_params=_cparams(("parallel",)),
        name="in_proj",
    )(x2, g, w, rc, rs1, rs2)


def _banded_kernel(q_ref, k_ref, v_ref, o_ref, lse_ref, *, seq_len, q_block, sub, k_win):
    i = pl.program_id(1)
    rows = N_HEADS * sub
    lane = lax.broadcasted_iota(jnp.int32, (rows, A_WIDTH), 1)
    row_head = lax.broadcasted_iota(jnp.int32, (rows, A_WIDTH), 0) // sub
    own_lanes = (lane // HEAD_DIM) == row_head
    rel = (lax.broadcasted_iota(jnp.int32, (rows, k_win), 1)
           - lax.broadcasted_iota(jnp.int32, (rows, k_win), 0) % sub)
    head_lane = lax.broadcasted_iota(jnp.int32, (sub, A_WIDTH), 1) // HEAD_DIM
    n_sub = q_block // sub
    q0s = [i * q_block + sb * sub for sb in range(n_sub)]
    kss = [pl.multiple_of(jnp.clip(q0 - BAND_HALF, 0, seq_len - k_win), BAND_HALF) for q0 in q0s]

    def score(sb):
        qblk = q_ref[0, sb * sub:(sb + 1) * sub, :]
        q4 = jnp.concatenate([qblk] * N_HEADS, axis=0)
        q4 = jnp.where(own_lanes, q4, jnp.zeros_like(q4))
        return _dot_nt(q4, k_ref[0, pl.ds(kss[sb], k_win), :])

    def softmax(sb, s):
        band = jnp.abs(rel + (kss[sb] - q0s[sb])) <= BAND_HALF
        s = jnp.where(band, s, NEG_INF)
        m = jnp.max(s, axis=-1, keepdims=True)
        p = jnp.exp2(s - m)
        den = jnp.sum(p, axis=-1, keepdims=True)
        return p.astype(BF16), den, (m + jnp.log2(den)) * LN2

    def per_head(x4):
        out = x4[(N_HEADS - 1) * sub:N_HEADS * sub]
        for h in reversed(range(N_HEADS - 1)):
            out = jnp.where(head_lane == h, x4[h * sub:(h + 1) * sub], out)
        return out

    ss = [score(sb) for sb in range(n_sub)]
    sm = [softmax(sb, s) for sb, s in enumerate(ss)]
    pv = [_dot(p, v_ref[0, pl.ds(kss[sb], k_win), :]) for sb, (p, _, _) in enumerate(sm)]
    for sb in range(n_sub):
        _, den, lse = sm[sb]
        o_ref[0, sb * sub:(sb + 1) * sub, :] = per_head(pv[sb]) / per_head(den)
        lse_ref[0, sb * sub:(sb + 1) * sub, :] = per_head(lse)


def _banded(q, k, v):
    n_seq, seq_len, w = q.shape
    q_block = min(2 * MXU_TILE, seq_len)
    sub = min(LANES, seq_len)
    k_win = min(sub + 2 * BAND_HALF, seq_len)
    kern = functools.partial(_banded_kernel, seq_len=seq_len, q_block=q_block, sub=sub, k_win=k_win)
    full = pl.BlockSpec((1, seq_len, w), lambda s, i: (s, 0, 0))
    blk = pl.BlockSpec((1, q_block, w), lambda s, i: (s, i, 0))
    return pl.pallas_call(
        kern,
        grid=(n_seq, seq_len // q_block),
        in_specs=[blk, full, full],
        out_specs=[blk, blk],
        out_shape=[jax.ShapeDtypeStruct((n_seq, seq_len, w), F32)] * 2,
        compiler_params=_cparams(("parallel", "arbitrary")),
        name="banded_attn",
    )(q, k, v)


def _split3(x):
    hi = x.astype(BF16)
    r1 = x - hi.astype(F32)
    mid = r1.astype(BF16)
    lo = (r1 - mid.astype(F32)).astype(BF16)
    return hi, mid, lo


def _hgrn_kernel(lbl_ref, lbd_ref, qf_ref, zf_ref, vf_ref, qb_ref, zb_ref, vb_ref,
                 of_ref, ob_ref, sf_ref, sb_ref, *, depth, layer, n_chunks):
    c = HGRN_CHUNK
    rows = n_chunks * c
    group = lbd_ref.shape[0]

    @pl.when(pl.program_id(2) == 0)
    def _():
        sf_ref[...] = jnp.zeros_like(sf_ref)
        sb_ref[...] = jnp.zeros_like(sb_ref)

    def lower_bound(logits):
        e = jnp.exp(logits - jnp.max(logits, axis=0, keepdims=True))
        p = e / jnp.sum(e, axis=0, keepdims=True)
        lb = jnp.zeros((1, LANES), F32)
        for j in range(1, layer + 1):
            lb = lb + p[j:j + 1, :]
        return lb

    lb_f = lower_bound(lbl_ref[0:depth, :])
    lb_b = lower_bound(lbl_ref[depth:2 * depth, :])

    ti = lax.broadcasted_iota(jnp.int32, (c, c), 0)
    si = lax.broadcasted_iota(jnp.int32, (c, c), 1)
    tril = si <= ti
    triu = si >= ti
    lane_c = lax.broadcasted_iota(jnp.int32, (c, LANES), 1) < HEAD_DIM
    lane_b = lax.broadcasted_iota(jnp.int32, (rows, LANES), 1) < HEAD_DIM
    er = lax.broadcasted_iota(jnp.int32, (LANES, LANES), 0) < HEAD_DIM
    ec = lax.broadcasted_iota(jnp.int32, (LANES, LANES), 1) < HEAD_DIM
    same_head = er == ec
    lbd = lbd_ref[...]

    def cumsum(logf):
        parts = jnp.concatenate(_split3(logf), axis=1)
        outs = []
        for g in range(rows // group):
            r = _dot(lbd, parts[g * group:(g + 1) * group, :])
            outs.append(r[:, 0:LANES] + r[:, LANES:2 * LANES] + r[:, 2 * LANES:3 * LANES])
        return jnp.concatenate(outs, axis=0)

    def chunk_rows(x, off):
        return jnp.concatenate(
            [jnp.broadcast_to(x[j * c + off:j * c + off + 1, :], (c, LANES)) for j in range(n_chunks)], axis=0)

    rs = [slice(j * c, (j + 1) * c) for j in range(n_chunks)]

    def prepare(q_ref, z_ref, v_ref, lb, forward):
        q = q_ref[...].astype(F32)
        z = z_ref[...]
        v = v_ref[...]
        logf = jnp.log(lb + (1.0 - lb) * jax.nn.sigmoid(z))
        kk = (1.0 - lb) * jax.nn.sigmoid(-z)
        a = cumsum(logf)
        last = chunk_rows(a, c - 1)
        if forward:
            e = a
            mid = chunk_rows(a, c // 2 - 1)
            q_in = q * jnp.exp(a)
            k_st = kk * jnp.exp(last - a)
            tri = tril
        else:
            e = a - logf
            mid = chunk_rows(e, c // 2)
            q_in = q * jnp.exp(last - e)
            k_st = kk * jnp.exp(e)
            tri = triu
        sgn = 1.0 if forward else -1.0
        qd = q * jnp.exp(jnp.minimum(sgn * (e - mid), EXP_CLAMP))
        kd = (kk * jnp.exp(jnp.minimum(sgn * (mid - e), EXP_CLAMP))).astype(BF16)
        q0 = jnp.where(lane_b, qd, 0.0).astype(BF16)
        q1 = jnp.where(lane_b, 0.0, qd).astype(BF16)
        decay = [jnp.exp(a[(j + 1) * c - 1:(j + 1) * c, :]) for j in range(n_chunks)]
        return dict(q0=q0, q1=q1, kd=kd, v=v, q_in=q_in.astype(BF16), k_st=k_st.astype(BF16), tri=tri,
                    decay=decay, forward=forward)

    def first_dots(d):
        return ([_dot_nt(d["q0"][r], d["kd"][r]) for r in rs], [_dot_nt(d["q1"][r], d["kd"][r]) for r in rs],
                [_dot_tn(d["v"][r], d["k_st"][r]) for r in rs])

    def intra_dots(d, s0, s1):
        s0 = [jnp.where(d["tri"], s, 0.0).astype(BF16) for s in s0]
        s1 = [jnp.where(d["tri"], s, 0.0).astype(BF16) for s in s1]
        return [jnp.where(lane_c, _dot(x0, d["v"][r]), _dot(x1, d["v"][r])) for x0, x1, r in zip(s0, s1, rs)]

    def scan(d, ut, st_ref):
        st = st_ref[...]
        states = [None] * n_chunks
        for j in (range(n_chunks) if d["forward"] else reversed(range(n_chunks))):
            states[j] = st.astype(BF16)
            st = st * d["decay"][j] + jnp.where(same_head, ut[j], 0.0)
        st_ref[...] = st
        return states

    def finish(d, states, intra, o_ref):
        for j, r in enumerate(rs):
            o_ref[r, :] = _dot_nt(d["q_in"][r], states[j]) + intra[j]

    df = prepare(qf_ref, zf_ref, vf_ref, lb_f, True)
    db = prepare(qb_ref, zb_ref, vb_ref, lb_b, False)
    f0, f1, fu = first_dots(df)
    b0, b1, bu = first_dots(db)
    fi = intra_dots(df, f0, f1)
    bi = intra_dots(db, b0, b1)
    fs = scan(df, fu, sf_ref)
    bs = scan(db, bu, sb_ref)
    finish(df, fs, fi, of_ref)
    finish(db, bs, bi, ob_ref)


def _hgrn(lb_logits2, q, zf, zb, v, batch, seq, layer, rows):
    t = q.shape[0]
    depth = lb_logits2.shape[0] // 2
    n = seq // rows
    fwd = lambda b, p, i: (b * n + i, p)
    bwd = lambda b, p, i: (b * n + (n - 1 - i), p)
    kern = functools.partial(_hgrn_kernel, depth=depth, layer=layer, n_chunks=rows // HGRN_CHUNK)
    spec = lambda im: pl.BlockSpec((rows, LANES), im)
    group = min(256, rows)
    idx = jnp.arange(group)
    lbd = ((idx[None, :] <= idx[:, None])
           & (idx[None, :] // HGRN_CHUNK == idx[:, None] // HGRN_CHUNK)).astype(BF16)
    return pl.pallas_call(
        kern,
        grid=(batch, B_WIDTH // LANES, n),
        in_specs=[pl.BlockSpec((2 * depth, LANES), lambda b, p, i: (0, p)),
                  pl.BlockSpec((group, group), lambda b, p, i: (0, 0)),
                  spec(fwd), spec(fwd), spec(fwd), spec(bwd), spec(bwd), spec(bwd)],
        out_specs=[spec(fwd), spec(bwd)],
        out_shape=[jax.ShapeDtypeStruct((t, B_WIDTH), F32)] * 2,
        scratch_shapes=[pltpu.VMEM((LANES, LANES), F32), pltpu.VMEM((LANES, LANES), F32)],
        compiler_params=_cparams(("parallel", "parallel", "arbitrary")),
        name="hgrn2",
    )(lb_logits2, lbd, q, zf, v, q, zb, v)


def _diff_kernel(lam_ref, g_ref, q0_ref, q1_ref, q2_ref, k_ref, vt_ref, o_ref,
                 s1_ref, s2_ref, p1_ref, p2_ref, st_ref, *, seq, tq, kc, unroll, lam_init):
    lp = lam_ref[...]
    lam = (jnp.exp(jnp.sum(lp[0:1, :] * lp[1:2, :], axis=1, keepdims=True))
           - jnp.exp(jnp.sum(lp[2:3, :] * lp[3:4, :], axis=1, keepdims=True)) + lam_init)
    n = seq // kc
    sub = 8

    def split(qt):
        row = lax.broadcasted_iota(jnp.int32, qt.shape, 0)
        zero = jnp.zeros_like(qt)
        return jnp.where(row < HEAD_DIM, qt, zero), jnp.where(row < HEAD_DIM, zero, qt)

    def scores(k0, q, s_ref, mrun):
        s = _dot(k_ref[pl.ds(k0, kc), :], q)
        s_ref[pl.ds(k0, kc), :] = s
        return jnp.maximum(mrun, jnp.max(s.reshape(kc // sub, sub, tq), axis=0))

    def expo(k0, s_ref, p_ref, m, lrun):
        p = jnp.exp2(s_ref[pl.ds(k0, kc), :] - m)
        p_ref[pl.ds(k0, kc), :] = p.astype(BF16)
        return lrun + jnp.sum(p.reshape(kc // sub, sub, tq), axis=0)

    part = lambda val: jnp.full((sub, tq), val, F32)
    loop = functools.partial(lax.fori_loop, 0, n, unroll=unroll)
    fill_loop = functools.partial(lax.fori_loop, 0, n, unroll=min(4, n))
    rows = lambda a: slice(a * sub, (a + 1) * sub)
    colmax = lambda a: jnp.max(st_ref[rows(a), :], axis=0, keepdims=True)
    colsum = lambda a: jnp.sum(st_ref[rows(a), :], axis=0, keepdims=True)

    @pl.when(pl.program_id(2) == 0)
    def _():
        qa1, qa2 = split(q0_ref[0])

        def fill_s(j, carry):
            k0 = pl.multiple_of(j * kc, kc)
            return scores(k0, qa1, s1_ref, carry[0]), scores(k0, qa2, s2_ref, carry[1])

        mr1, mr2 = fill_loop(fill_s, (part(NEG_INF), part(NEG_INF)))
        ma1 = jnp.max(mr1, axis=0, keepdims=True)
        ma2 = jnp.max(mr2, axis=0, keepdims=True)
        qb1, qb2 = split(q1_ref[0])

        def fill_e(j, carry):
            l1, l2, mr1, mr2 = carry
            k0 = pl.multiple_of(j * kc, kc)
            l1 = expo(k0, s1_ref, p1_ref, ma1, l1)
            mr1 = scores(k0, qb1, s1_ref, mr1)
            l2 = expo(k0, s2_ref, p2_ref, ma2, l2)
            mr2 = scores(k0, qb2, s2_ref, mr2)
            return l1, l2, mr1, mr2

        l1, l2, mr1, mr2 = fill_loop(fill_e, (part(0.0), part(0.0), part(NEG_INF), part(NEG_INF)))
        st_ref[rows(0), :] = mr1
        st_ref[rows(1), :] = mr2
        st_ref[rows(2), :] = l1
        st_ref[rows(3), :] = l2

    m1, m2 = colmax(0), colmax(1)
    l1, l2 = colsum(2), colsum(3)
    c16 = jnp.broadcast_to((lam * l1 / l2).astype(BF16), (BF16_ROWS, tq))
    qs1, qs2 = split(q2_ref[0])

    def body(j, carry):
        acc, e1, e2, mr1, mr2 = carry
        k0 = pl.multiple_of(j * kc, kc)
        p2 = p2_ref[pl.ds(k0, kc), :].reshape(kc // BF16_ROWS, BF16_ROWS, tq)
        w = p1_ref[pl.ds(k0, kc), :] - (p2 * c16).reshape(kc, tq)
        acc = acc + _dot(vt_ref[0, :, pl.ds(k0, kc)], w)
        e1 = expo(k0, s1_ref, p1_ref, m1, e1)
        mr1 = scores(k0, qs1, s1_ref, mr1)
        e2 = expo(k0, s2_ref, p2_ref, m2, e2)
        mr2 = scores(k0, qs2, s2_ref, mr2)
        return acc, e1, e2, mr1, mr2

    acc, e1, e2, mr1, mr2 = loop(body, (jnp.zeros((2 * HEAD_DIM, tq), F32), part(0.0), part(0.0),
                                        part(NEG_INF), part(NEG_INF)))
    st_ref[rows(0), :] = mr1
    st_ref[rows(1), :] = mr2
    st_ref[rows(2), :] = e1
    st_ref[rows(3), :] = e2
    o = acc / l1
    ms = jnp.mean(o * o, axis=0, keepdims=True)
    y = o * lax.rsqrt(ms + EPS) * g_ref[...] * (1.0 - lam_init)
    o_ref[...] = y.T.astype(o_ref.dtype)


def _diff_attn(lam_p, g_col, qct, kc_nat, vct, batch, seq, layer, tq, kc):
    t = kc_nat.shape[0]
    lam_init = 0.8 - 0.6 * math.exp(-0.3 * layer)
    nq = seq // tq
    kern = functools.partial(_diff_kernel, seq=seq, tq=tq, kc=kc, unroll=min(16, seq // kc), lam_init=lam_init)
    q_spec = lambda ahead: pl.BlockSpec((1, 2 * HEAD_DIM, tq),
                                        lambda b, h, i: (b, h, jnp.minimum(i + ahead, nq - 1)))
    return pl.pallas_call(
        kern,
        grid=(batch, N_HEADS, nq),
        in_specs=[pl.BlockSpec(lam_p.shape, lambda b, h, i: (0, 0)),
                  pl.BlockSpec((2 * HEAD_DIM, 1), lambda b, h, i: (0, 0)),
                  q_spec(0), q_spec(1), q_spec(2),
                  pl.BlockSpec((seq, 2 * HEAD_DIM), lambda b, h, i: (b, h)),
                  pl.BlockSpec((1, 2 * HEAD_DIM, seq), lambda b, h, i: (b, h, 0))],
        out_specs=pl.BlockSpec((tq, 2 * HEAD_DIM), lambda b, h, i: (b * nq + i, h)),
        out_shape=jax.ShapeDtypeStruct((t, C_WIDTH), BF16),
        scratch_shapes=[pltpu.VMEM((seq, tq), F32), pltpu.VMEM((seq, tq), F32),
                        pltpu.VMEM((seq, tq), BF16), pltpu.VMEM((seq, tq), BF16),
                        pltpu.VMEM((32, tq), F32)],
        compiler_params=_cparams(("parallel", "parallel", "arbitrary")),
        name="diff_attn",
    )(lam_p, g_col, qct, qct, qct, kc_nat, vct)


def _out_proj_kernel(x_ref, o1_ref, l1_ref, o2_ref, l2_ref, o3_ref, l3_ref,
                     of_ref, ob_ref, gb_ref, oc_ref, hg_ref, w_ref, g_ref, h_ref, *stage_refs):
    tm = x_ref.shape[0]
    ab_width = A_WIDTH + B_WIDTH
    mix_c = _dot(oc_ref[...], w_ref[ab_width:, :])

    def natural(ref, dil, stage_ref):
        halves = range(A_WIDTH // LANES)
        for r in range(dil):
            blk = ref[0, r]
            for h in halves:
                stage_ref[h, pl.ds(r, tm // dil, stride=dil), :] = blk[:, h * LANES:(h + 1) * LANES]
        return jnp.concatenate([stage_ref[h] for h in halves], axis=1)

    o2, l2 = natural(o2_ref, DILATIONS[1], stage_refs[0]), natural(l2_ref, DILATIONS[1], stage_refs[1])
    o3, l3 = natural(o3_ref, DILATIONS[2], stage_refs[2]), natural(l3_ref, DILATIONS[2], stage_refs[3])
    l1 = l1_ref[...]
    mx = jnp.maximum(jnp.maximum(l1, l2), l3)
    w1, w2, w3 = jnp.exp(l1 - mx), jnp.exp(l2 - mx), jnp.exp(l3 - mx)
    oa = (w1 * o1_ref[...] + w2 * o2 + w3 * o3) / (w1 + w2 + w3)

    y = of_ref[...] + ob_ref[...]
    y2 = y * y
    lane = lax.broadcasted_iota(jnp.int32, y.shape, 1)
    ms = jnp.zeros_like(y)
    for h in range(N_HEADS):
        in_head = (lane >= h * HEAD_DIM) & (lane < (h + 1) * HEAD_DIM)
        ms_h = jnp.sum(jnp.where(in_head, y2, 0.0), axis=-1, keepdims=True) * (1.0 / HEAD_DIM)
        ms = jnp.where(in_head, ms_h, ms)
    gb = gb_ref[...]
    ob = y * lax.rsqrt(ms + EPS) * hg_ref[...] * (gb * jax.nn.sigmoid(gb))

    cat = jnp.concatenate([oa.astype(BF16), ob.astype(BF16)], axis=1)
    mix = _dot(cat, w_ref[0:ab_width, :]) + mix_c
    h_ref[...] = x_ref[...] + _rms(mix, g_ref[...])


def _out_proj(x2, a_outs, of, ob, gb, oc, hg, w, g, seq, tm):
    t, d = x2.shape
    tiles_per_seq = seq // tm
    row = lambda i: (i, 0)
    const = lambda i: (0, 0)
    r256 = pl.BlockSpec((tm, 256), row)
    sub = lambda dil: pl.BlockSpec((1, dil, tm // dil, A_WIDTH),
                                   lambda i: (i // tiles_per_seq, 0, i % tiles_per_seq, 0))
    a_specs = [r256, r256] + [sub(DILATIONS[1])] * 2 + [sub(DILATIONS[2])] * 2
    return pl.pallas_call(
        _out_proj_kernel,
        grid=(t // tm,),
        in_specs=[pl.BlockSpec((tm, d), row)] + a_specs + [r256] * 3 + [
            pl.BlockSpec((tm, C_WIDTH), row), pl.BlockSpec((1, B_WIDTH), const),
            pl.BlockSpec(w.shape, const), pl.BlockSpec((1, d), const)],
        out_specs=pl.BlockSpec((tm, d), row),
        out_shape=jax.ShapeDtypeStruct((t, d), F32),
        scratch_shapes=[pltpu.VMEM((A_WIDTH // LANES, tm, LANES), F32)] * 4,
        compiler_params=_cparams(("parallel",)),
        name="out_proj",
    )(x2, *a_outs, of, ob, gb, oc, hg, w, g)


def _ffn_kernel(hp_ref, h_ref, hn_ref, gpre_ref, wup_ref, cw_ref, cb_ref, wdn_ref, gpost_ref,
                o_ref, xe_ref, act_ref, *, tm, tiles_per_seq, d_ff, cn):
    i = pl.program_id(0)
    halo = BF16_ROWS
    first = (i % tiles_per_seq) == 0
    last = (i % tiles_per_seq) == tiles_per_seq - 1
    gpre = gpre_ref[...]
    h = h_ref[...]
    xe_ref[halo:halo + tm, :] = _rms(h, gpre).astype(BF16)
    xe_ref[0:halo, :] = jnp.where(first, 0.0, _rms(hp_ref[...], gpre)).astype(BF16)
    xe_ref[halo + tm:2 * halo + tm, :] = jnp.where(last, 0.0, _rms(hn_ref[...], gpre)).astype(BF16)
    xe = xe_ref[...]

    def conv(c0):
        u = _dot(xe, wup_ref[:, c0:c0 + cn])
        cw = cw_ref[:, c0:c0 + cn]
        return (u[halo - 1:halo - 1 + tm] * cw[0:1] + u[halo:halo + tm] * cw[1:2]
                + u[halo + 1:halo + 1 + tm] * cw[2:3] + cb_ref[:, c0:c0 + cn])

    for c in range(d_ff // cn):
        gate = conv(c * cn)
        val = conv(d_ff + c * cn)
        act_ref[:, c * cn:(c + 1) * cn] = (gate * jax.nn.sigmoid(gate) * val).astype(BF16)
    ff = _dot(act_ref[...], wdn_ref[...])
    o_ref[...] = h + _rms(ff, gpost_ref[...])


def _ffn(h2, gpre, wup, cw, cb, wdn, gpost, seq, tm, cn):
    t, d = h2.shape
    d_ff = wdn.shape[0]
    halo = BF16_ROWS
    tiles_per_seq = seq // tm
    hb = tm // halo
    const = lambda i: (0, 0)
    kern = functools.partial(_ffn_kernel, tm=tm, tiles_per_seq=tiles_per_seq, d_ff=d_ff, cn=cn)
    return pl.pallas_call(
        kern,
        grid=(t // tm,),
        in_specs=[pl.BlockSpec((halo, d), lambda i: (jnp.maximum(i * hb - 1, 0), 0)),
                  pl.BlockSpec((tm, d), lambda i: (i, 0)),
                  pl.BlockSpec((halo, d), lambda i: (jnp.minimum((i + 1) * hb, t // halo - 1), 0)),
                  pl.BlockSpec((1, d), const),
                  pl.BlockSpec(wup.shape, const), pl.BlockSpec(cw.shape, const),
                  pl.BlockSpec(cb.shape, const), pl.BlockSpec(wdn.shape, const),
                  pl.BlockSpec((1, d), const)],
        out_specs=pl.BlockSpec((tm, d), lambda i: (i, 0)),
        out_shape=jax.ShapeDtypeStruct((t, d), F32),
        scratch_shapes=[pltpu.VMEM((tm + 2 * halo, d), BF16), pltpu.VMEM((tm, d_ff), BF16)],
        compiler_params=_cparams(("parallel",)),
        name="conv_ffn",
    )(h2, h2, h2, gpre, wup, cw, cb, wdn, gpost)


def _rope_tables(seq):
    pos = jnp.arange(seq, dtype=F32)
    inv = ROPE_THETA ** (-jnp.arange(0, ROPE_DIM, 2, dtype=F32) / ROPE_DIM)
    ang = pos[:, None] * inv[None, :]
    cos, sin = jnp.cos(ang), jnp.sin(ang)
    rest = HEAD_DIM - ROPE_DIM
    one, zero = jnp.ones((seq, rest), F32), jnp.zeros((seq, rest), F32)
    zh = jnp.zeros((seq, ROPE_HALF), F32)
    rep = LANES // HEAD_DIM
    rc = jnp.tile(jnp.concatenate([cos, cos, one], axis=1), (1, rep))
    rs1 = jnp.tile(jnp.concatenate([-sin, zh, zero], axis=1), (1, rep))
    rs2 = jnp.tile(jnp.concatenate([zh, sin, zero], axis=1), (1, rep))
    return rc, rs1, rs2


class _Tiles(NamedTuple):
    rows: int
    q_tile: int
    key_chunk: int
    ff_chunk: int


def _tiles(seq):
    assert seq % (DILATIONS[-1] * BF16_ROWS) == 0 and seq % HGRN_CHUNK == 0, seq
    rows = min(2 * MXU_TILE, seq)
    assert seq % rows == 0 and rows % (DILATIONS[-1] * BF16_ROWS) == 0, (seq, rows)
    return _Tiles(rows=rows, q_tile=min(MXU_TILE, seq), key_chunk=min(2 * MXU_TILE, seq), ff_chunk=MXU_TILE)


def kernel(x, w_in, w_out, lb_logits, hgrn_norm, diff_lambda, diff_norm, w_up, conv_w, conv_b,
           w_down, norm_pre_mix, norm_post_mix, norm_pre_ffn, norm_post_ffn):
    batch, seq, d = x.shape
    depth = w_in.shape[0]
    tiles = _tiles(seq)
    tm = tiles.rows
    rc, rs1, rs2 = _rope_tables(seq)
    lbl2 = lb_logits.astype(F32).reshape(2 * depth, B_WIDTH)
    x2 = x.reshape(batch * seq, d)
    for l in range(depth):
        (qa, ka, va, qa4, ka4, va4, qa16, ka16, va16, qb, zf, zb, ib, gb, qct, kc, vct) = _in_proj(
            x2, norm_pre_mix[l].reshape(1, d), w_in[l].astype(BF16), rc, rs1, rs2, batch, seq, tm)

        a_outs = []
        for dil, qkv in zip(DILATIONS, ((qa, ka, va), (qa4, ka4, va4), (qa16, ka16, va16))):
            o, lse = _banded(*(a.reshape(batch * dil, seq // dil, A_WIDTH) for a in qkv))
            shape = (batch * seq, A_WIDTH) if dil == 1 else (batch, dil, seq // dil, A_WIDTH)
            a_outs += [o.reshape(shape), lse.reshape(shape)]

        of, ob = _hgrn(lbl2, qb, zf, zb, ib, batch, seq, l, rows=tm)

        oc = _diff_attn(diff_lambda[l].astype(F32), diff_norm[l].astype(F32).reshape(2 * HEAD_DIM, 1),
                        qct, kc, vct, batch, seq, l, tq=tiles.q_tile, kc=tiles.key_chunk)

        hg = jnp.tile(hgrn_norm[l].astype(F32), N_HEADS).reshape(1, B_WIDTH)
        h2 = _out_proj(x2, a_outs, of, ob, gb, oc, hg, w_out[l].astype(BF16),
                       norm_post_mix[l].reshape(1, d), seq, tm)

        x2 = _ffn(h2, norm_pre_ffn[l].reshape(1, d), w_up[l].astype(BF16), conv_w[l],
                  conv_b[l].reshape(1, -1), w_down[l].astype(BF16), norm_post_ffn[l].reshape(1, d),
                  seq, tm, cn=tiles.ff_chunk)
    return x2.reshape(batch, seq, d)
```

```python
import functools
import math
from typing import NamedTuple

import jax
import jax.numpy as jnp
from jax import lax
from jax.experimental import pallas as pl
from jax.experimental.pallas import tpu as pltpu

F32 = jnp.float32
BF16 = jnp.bfloat16

HEAD_DIM = 64
N_HEADS = 4
A_WIDTH = N_HEADS * HEAD_DIM
B_WIDTH = N_HEADS * HEAD_DIM
C_WIDTH = N_HEADS * 2 * HEAD_DIM
ROPE_THETA = 500000.0
ROPE_DIM = HEAD_DIM // 4
ROPE_HALF = ROPE_DIM // 2
DILATIONS = (1, 4, 16)
BAND_HALF = 64
HGRN_CHUNK = 64
CONV_WIDTH = 3
EPS = 1e-6
NEG_INF = -1e30
LOG2E = math.log2(math.e)
LN2 = math.log(2.0)
EXP_CLAMP = 80.0

LANES = 128
BF16_ROWS = 16
MXU_TILE = 256
VMEM_LIMIT = 56 * 1024 * 1024

OFF_QA, OFF_KA, OFF_VA = 0, 256, 512
OFF_QB, OFF_ZF, OFF_ZB, OFF_IB, OFF_GB = 768, 1024, 1280, 1536, 1792
OFF_QC, OFF_KC, OFF_VC = 2048, 2560, 3072


def _cparams(sem):
    return pltpu.CompilerParams(dimension_semantics=sem, vmem_limit_bytes=VMEM_LIMIT)


def _rms(x, g):
    ms = jnp.mean(x * x, axis=-1, keepdims=True)
    return x * lax.rsqrt(ms + EPS) * g


def _dot(a, b):
    return jnp.dot(a, b, preferred_element_type=F32)


def _dot_nt(a, b):
    return lax.dot_general(a, b, (((1,), (1,)), ((), ())), preferred_element_type=F32)


def _dot_tn(a, b):
    return lax.dot_general(a, b, (((0,), (0,)), ((), ())), preferred_element_type=F32)


def _in_proj_kernel(x_ref, g_ref, w_ref, rc_ref, rs1_ref, rs2_ref,
                    qa_ref, ka_ref, va_ref, qa4_ref, ka4_ref, va4_ref, qa16_ref, ka16_ref, va16_ref,
                    qb_ref, zf_ref, zb_ref, ib_ref, gb_ref, qct_ref, kc_ref, vct_ref,
                    sq_ref, sk_ref, sv_ref):
    tm = x_ref.shape[0]
    xn = _rms(x_ref[...], g_ref[...]).astype(BF16)

    def emit_a(val, stage_ref, nat_ref, sub_refs):
        nat_ref[...] = val.astype(BF16)
        halves = range(A_WIDTH // LANES)
        for h in halves:
            stage_ref[h] = val[:, h * LANES:(h + 1) * LANES]
        for dil, ref in zip(DILATIONS[1:], sub_refs):
            for r in range(dil):
                ref[0, r, :, :] = jnp.concatenate(
                    [stage_ref[h, pl.ds(r, tm // dil, stride=dil), :] for h in halves], axis=1).astype(BF16)

    rc, rs1, rs2 = rc_ref[...], rs1_ref[...], rs2_ref[...]

    def proj(c0, n):
        return _dot(xn, w_ref[:, c0:c0 + n])

    def rope(a):
        outs = []
        for j in range(a.shape[1] // LANES):
            blk = a[:, j * LANES:(j + 1) * LANES]
            outs.append(blk * rc + pltpu.roll(blk, LANES - ROPE_HALF, 1) * rs1
                        + pltpu.roll(blk, ROPE_HALF, 1) * rs2)
        return outs

    def cat(blks):
        return jnp.concatenate(blks, axis=1)

    scale = 1.0 / math.sqrt(HEAD_DIM)
    for half in range(2):
        qblks = rope(proj(OFF_QC + 256 * half, 256))
        kblks = rope(proj(OFF_KC + 256 * half, 256))
        v = proj(OFF_VC + 256 * half, 256)
        kc_ref[:, 256 * half:256 * (half + 1)] = cat(kblks).astype(BF16)
        for j in range(2):
            r0 = 256 * half + LANES * j
            qct_ref[0, r0:r0 + LANES, :] = (qblks[j] * (scale * LOG2E)).T.astype(BF16)
            vct_ref[0, r0:r0 + LANES, :] = v[:, j * LANES:(j + 1) * LANES].T.astype(BF16)
    emit_a(cat(rope(proj(OFF_QA, 256))) * (scale * LOG2E), sq_ref, qa_ref, (qa4_ref, qa16_ref))
    emit_a(cat(rope(proj(OFF_KA, 256))), sk_ref, ka_ref, (ka4_ref, ka16_ref))
    emit_a(proj(OFF_VA, 256), sv_ref, va_ref, (va4_ref, va16_ref))
    qb_ref[...] = proj(OFF_QB, 256).astype(BF16)
    zf_ref[...] = proj(OFF_ZF, 256)
    zb_ref[...] = proj(OFF_ZB, 256)
    ib_ref[...] = proj(OFF_IB, 256).astype(BF16)
    gb_ref[...] = proj(OFF_GB, 256)


def _in_proj(x2, g, w, rc, rs1, rs2, batch, seq, tm):
    t, d = x2.shape
    n_cols = w.shape[1]
    tiles_per_seq = seq // tm
    row = lambda i: (i, 0)
    const = lambda i: (0, 0)
    ropei = lambda i: (i % tiles_per_seq, 0)
    tr = lambda i: (i // tiles_per_seq, 0, i % tiles_per_seq)
    nat = lambda n, dt: jax.ShapeDtypeStruct((t, n), dt)
    sub_shape = lambda dil: jax.ShapeDtypeStruct((batch, dil, seq // dil, A_WIDTH), BF16)
    sub_spec = lambda dil: pl.BlockSpec((1, dil, tm // dil, A_WIDTH),
                                        lambda i: (i // tiles_per_seq, 0, i % tiles_per_seq, 0))
    out_shape = ([nat(256, BF16)] * 3
                 + [sub_shape(DILATIONS[1])] * 3 + [sub_shape(DILATIONS[2])] * 3
                 + [nat(256, BF16), nat(256, F32), nat(256, F32), nat(256, BF16), nat(256, F32)]
                 + [jax.ShapeDtypeStruct((batch, C_WIDTH, seq), BF16),
                    nat(C_WIDTH, BF16),
                    jax.ShapeDtypeStruct((batch, C_WIDTH, seq), BF16)])
    out_specs = ([pl.BlockSpec((tm, 256), row)] * 3 + [sub_spec(DILATIONS[1])] * 3 + [sub_spec(DILATIONS[2])] * 3
                 + [pl.BlockSpec((tm, 256), row)] * 5
                 + [pl.BlockSpec((1, C_WIDTH, tm), tr), pl.BlockSpec((tm, C_WIDTH), row),
                    pl.BlockSpec((1, C_WIDTH, tm), tr)])
    return pl.pallas_call(
        _in_proj_kernel,
        grid=(t // tm,),
        in_specs=[pl.BlockSpec((tm, d), row), pl.BlockSpec((1, d), const),
                  pl.BlockSpec((d, n_cols), const),
                  pl.BlockSpec((tm, LANES), ropei), pl.BlockSpec((tm, LANES), ropei),
                  pl.BlockSpec((tm, LANES), ropei)],
        out_specs=out_specs,
        out_shape=out_shape,
        scratch_shapes=[pltpu.VMEM((A_WIDTH // LANES, tm, LANES), F32)] * 3,
        compiler_params=_cparams(("parallel",)),
        name="in_proj",
    )(x2, g, w, rc, rs1, rs2)


def _banded_kernel(q_ref, k_ref, v_ref, o_ref, lse_ref, *, seq_len, q_block, sub, k_win):
    i = pl.program_id(1)
    rows = N_HEADS * sub
    lane = lax.broadcasted_iota(jnp.int32, (rows, A_WIDTH), 1)
    row_head = lax.broadcasted_iota(jnp.int32, (rows, A_WIDTH), 0) // sub
    own_lanes = (lane // HEAD_DIM) == row_head
    rel = (lax.broadcasted_iota(jnp.int32, (rows, k_win), 1)
           - lax.broadcasted_iota(jnp.int32, (rows, k_win), 0) % sub)
    head_lane = lax.broadcasted_iota(jnp.int32, (sub, A_WIDTH), 1) // HEAD_DIM
    n_sub = q_block // sub
    q0s = [i * q_block + sb * sub for sb in range(n_sub)]
    kss = [pl.multiple_of(jnp.clip(q0 - BAND_HALF, 0, seq_len - k_win), BAND_HALF) for q0 in q0s]

    def score(sb):
        qblk = q_ref[0, sb * sub:(sb + 1) * sub, :]
        q4 = jnp.concatenate([qblk] * N_HEADS, axis=0)
        q4 = jnp.where(own_lanes, q4, jnp.zeros_like(q4))
        return _dot_nt(q4, k_ref[0, pl.ds(kss[sb], k_win), :])

    def softmax(sb, s):
        band = jnp.abs(rel + (kss[sb] - q0s[sb])) <= BAND_HALF
        s = jnp.where(band, s, NEG_INF)
        m = jnp.max(s, axis=-1, keepdims=True)
        p = jnp.exp2(s - m)
        den = jnp.sum(p, axis=-1, keepdims=True)
        return p.astype(BF16), den, (m + jnp.log2(den)) * LN2

    def per_head(x4):
        out = x4[(N_HEADS - 1) * sub:N_HEADS * sub]
        for h in reversed(range(N_HEADS - 1)):
            out = jnp.where(head_lane == h, x4[h * sub:(h + 1) * sub], out)
        return out

    ss = [score(sb) for sb in range(n_sub)]
    sm = [softmax(sb, s) for sb, s in enumerate(ss)]
    pv = [_dot(p, v_ref[0, pl.ds(kss[sb], k_win), :]) for sb, (p, _, _) in enumerate(sm)]
    for sb in range(n_sub):
        _, den, lse = sm[sb]
        o_ref[0, sb * sub:(sb + 1) * sub, :] = per_head(pv[sb]) / per_head(den)
        lse_ref[0, sb * sub:(sb + 1) * sub, :] = per_head(lse)


def _banded(q, k, v):
    n_seq, seq_len, w = q.shape
    q_block = min(2 * MXU_TILE, seq_len)
    sub = min(LANES, seq_len)
    k_win = min(sub + 2 * BAND_HALF, seq_len)
    kern = functools.partial(_banded_kernel, seq_len=seq_len, q_block=q_block, sub=sub, k_win=k_win)
    full = pl.BlockSpec((1, seq_len, w), lambda s, i: (s, 0, 0))
    blk = pl.BlockSpec((1, q_block, w), lambda s, i: (s, i, 0))
    return pl.pallas_call(
        kern,
        grid=(n_seq, seq_len // q_block),
        in_specs=[blk, full, full],
        out_specs=[blk, blk],
        out_shape=[jax.ShapeDtypeStruct((n_seq, seq_len, w), F32)] * 2,
        compiler_params=_cparams(("parallel", "arbitrary")),
        name="banded_attn",
    )(q, k, v)


def _split3(x):
    hi = x.astype(BF16)
    r1 = x - hi.astype(F32)
    mid = r1.astype(BF16)
    lo = (r1 - mid.astype(F32)).astype(BF16)
    return hi, mid, lo


def _hgrn_kernel(lbl_ref, lbd_ref, qf_ref, zf_ref, vf_ref, qb_ref, zb_ref, vb_ref,
                 of_ref, ob_ref, sf_ref, sb_ref, *, depth, layer, n_chunks):
    c = HGRN_CHUNK
    rows = n_chunks * c
    group = lbd_ref.shape[0]

    @pl.when(pl.program_id(2) == 0)
    def _():
        sf_ref[...] = jnp.zeros_like(sf_ref)
        sb_ref[...] = jnp.zeros_like(sb_ref)

    def lower_bound(logits):
        e = jnp.exp(logits - jnp.max(logits, axis=0, keepdims=True))
        p = e / jnp.sum(e, axis=0, keepdims=True)
        lb = jnp.zeros((1, LANES), F32)
        for j in range(1, layer + 1):
            lb = lb + p[j:j + 1, :]
        return lb

    lb_f = lower_bound(lbl_ref[0:depth, :])
    lb_b = lower_bound(lbl_ref[depth:2 * depth, :])

    ti = lax.broadcasted_iota(jnp.int32, (c, c), 0)
    si = lax.broadcasted_iota(jnp.int32, (c, c), 1)
    tril = si <= ti
    triu = si >= ti
    lane_c = lax.broadcasted_iota(jnp.int32, (c, LANES), 1) < HEAD_DIM
    lane_b = lax.broadcasted_iota(jnp.int32, (rows, LANES), 1) < HEAD_DIM
    er = lax.broadcasted_iota(jnp.int32, (LANES, LANES), 0) < HEAD_DIM
    ec = lax.broadcasted_iota(jnp.int32, (LANES, LANES), 1) < HEAD_DIM
    same_head = er == ec
    lbd = lbd_ref[...]

    def cumsum(logf):
        parts = jnp.concatenate(_split3(logf), axis=1)
        outs = []
        for g in range(rows // group):
            r = _dot(lbd, parts[g * group:(g + 1) * group, :])
            outs.append(r[:, 0:LANES] + r[:, LANES:2 * LANES] + r[:, 2 * LANES:3 * LANES])
        return jnp.concatenate(outs, axis=0)

    def chunk_rows(x, off):
        return jnp.concatenate(
            [jnp.broadcast_to(x[j * c + off:j * c + off + 1, :], (c, LANES)) for j in range(n_chunks)], axis=0)

    rs = [slice(j * c, (j + 1) * c) for j in range(n_chunks)]

    def prepare(q_ref, z_ref, v_ref, lb, forward):
        q = q_ref[...].astype(F32)
        z = z_ref[...]
        v = v_ref[...]
        logf = jnp.log(lb + (1.0 - lb) * jax.nn.sigmoid(z))
        kk = (1.0 - lb) * jax.nn.sigmoid(-z)
        a = cumsum(logf)
        last = chunk_rows(a, c - 1)
        if forward:
            e = a
            mid = chunk_rows(a, c // 2 - 1)
            q_in = q * jnp.exp(a)
            k_st = kk * jnp.exp(last - a)
            tri = tril
        else:
            e = a - logf
            mid = chunk_rows(e, c // 2)
            q_in = q * jnp.exp(last - e)
            k_st = kk * jnp.exp(e)
            tri = triu
        sgn = 1.0 if forward else -1.0
        qd = q * jnp.exp(jnp.minimum(sgn * (e - mid), EXP_CLAMP))
        kd = (kk * jnp.exp(jnp.minimum(sgn * (mid - e), EXP_CLAMP))).astype(BF16)
        q0 = jnp.where(lane_b, qd, 0.0).astype(BF16)
        q1 = jnp.where(lane_b, 0.0, qd).astype(BF16)
        decay = [jnp.exp(a[(j + 1) * c - 1:(j + 1) * c, :]) for j in range(n_chunks)]
        return dict(q0=q0, q1=q1, kd=kd, v=v, q_in=q_in.astype(BF16), k_st=k_st.astype(BF16), tri=tri,
                    decay=decay, forward=forward)

    def first_dots(d):
        return ([_dot_nt(d["q0"][r], d["kd"][r]) for r in rs], [_dot_nt(d["q1"][r], d["kd"][r]) for r in rs],
                [_dot_tn(d["v"][r], d["k_st"][r]) for r in rs])

    def intra_dots(d, s0, s1):
        s0 = [jnp.where(d["tri"], s, 0.0).astype(BF16) for s in s0]
        s1 = [jnp.where(d["tri"], s, 0.0).astype(BF16) for s in s1]
        return [jnp.where(lane_c, _dot(x0, d["v"][r]), _dot(x1, d["v"][r])) for x0, x1, r in zip(s0, s1, rs)]

    def scan(d, ut, st_ref):
        st = st_ref[...]
        states = [None] * n_chunks
        for j in (range(n_chunks) if d["forward"] else reversed(range(n_chunks))):
            states[j] = st.astype(BF16)
            st = st * d["decay"][j] + jnp.where(same_head, ut[j], 0.0)
        st_ref[...] = st
        return states

    def finish(d, states, intra, o_ref):
        for j, r in enumerate(rs):
            o_ref[r, :] = _dot_nt(d["q_in"][r], states[j]) + intra[j]

    df = prepare(qf_ref, zf_ref, vf_ref, lb_f, True)
    db = prepare(qb_ref, zb_ref, vb_ref, lb_b, False)
    f0, f1, fu = first_dots(df)
    b0, b1, bu = first_dots(db)
    fi = intra_dots(df, f0, f1)
    bi = intra_dots(db, b0, b1)
    fs = scan(df, fu, sf_ref)
    bs = scan(db, bu, sb_ref)
    finish(df, fs, fi, of_ref)
    finish(db, bs, bi, ob_ref)


def _hgrn(lb_logits2, q, zf, zb, v, batch, seq, layer, rows):
    t = q.shape[0]
    depth = lb_logits2.shape[0] // 2
    n = seq // rows
    fwd = lambda b, p, i: (b * n + i, p)
    bwd = lambda b, p, i: (b * n + (n - 1 - i), p)
    kern = functools.partial(_hgrn_kernel, depth=depth, layer=layer, n_chunks=rows // HGRN_CHUNK)
    spec = lambda im: pl.BlockSpec((rows, LANES), im)
    group = min(256, rows)
    idx = jnp.arange(group)
    lbd = ((idx[None, :] <= idx[:, None])
           & (idx[None, :] // HGRN_CHUNK == idx[:, None] // HGRN_CHUNK)).astype(BF16)
    return pl.pallas_call(
        kern,
        grid=(batch, B_WIDTH // LANES, n),
        in_specs=[pl.BlockSpec((2 * depth, LANES), lambda b, p, i: (0, p)),
                  pl.BlockSpec((group, group), lambda b, p, i: (0, 0)),
                  spec(fwd), spec(fwd), spec(fwd), spec(bwd), spec(bwd), spec(bwd)],
        out_specs=[spec(fwd), spec(bwd)],
        out_shape=[jax.ShapeDtypeStruct((t, B_WIDTH), F32)] * 2,
        scratch_shapes=[pltpu.VMEM((LANES, LANES), F32), pltpu.VMEM((LANES, LANES), F32)],
        compiler_params=_cparams(("parallel", "parallel", "arbitrary")),
        name="hgrn2",
    )(lb_logits2, lbd, q, zf, v, q, zb, v)


def _diff_kernel(lam_ref, g_ref, q0_ref, q1_ref, q2_ref, k_ref, vt_ref, o_ref,
                 s1_ref, s2_ref, p1_ref, p2_ref, st_ref, *, seq, tq, kc, unroll, lam_init):
    lp = lam_ref[...]
    lam = (jnp.exp(jnp.sum(lp[0:1, :] * lp[1:2, :], axis=1, keepdims=True))
           - jnp.exp(jnp.sum(lp[2:3, :] * lp[3:4, :], axis=1, keepdims=True)) + lam_init)
    n = seq // kc
    sub = 8

    def split(qt):
        row = lax.broadcasted_iota(jnp.int32, qt.shape, 0)
        zero = jnp.zeros_like(qt)
        return jnp.where(row < HEAD_DIM, qt, zero), jnp.where(row < HEAD_DIM, zero, qt)

    def scores(k0, q, s_ref, mrun):
        s = _dot(k_ref[pl.ds(k0, kc), :], q)
        s_ref[pl.ds(k0, kc), :] = s
        return jnp.maximum(mrun, jnp.max(s.reshape(kc // sub, sub, tq), axis=0))

    def expo(k0, s_ref, p_ref, m, lrun):
        p = jnp.exp2(s_ref[pl.ds(k0, kc), :] - m)
        p_ref[pl.ds(k0, kc), :] = p.astype(BF16)
        return lrun + jnp.sum(p.reshape(kc // sub, sub, tq), axis=0)

    part = lambda val: jnp.full((sub, tq), val, F32)
    loop = functools.partial(lax.fori_loop, 0, n, unroll=unroll)
    fill_loop = functools.partial(lax.fori_loop, 0, n, unroll=min(4, n))
    rows = lambda a: slice(a * sub, (a + 1) * sub)
    colmax = lambda a: jnp.max(st_ref[rows(a), :], axis=0, keepdims=True)
    colsum = lambda a: jnp.sum(st_ref[rows(a), :], axis=0, keepdims=True)

    @pl.when(pl.program_id(0) == 0)
    def _():
        qa1, qa2 = split(q0_ref[0])

        def fill_s(j, carry):
            k0 = pl.multiple_of(j * kc, kc)
            return scores(k0, qa1, s1_ref, carry[0]), scores(k0, qa2, s2_ref, carry[1])

        mr1, mr2 = fill_loop(fill_s, (part(NEG_INF), part(NEG_INF)))
        ma1 = jnp.max(mr1, axis=0, keepdims=True)
        ma2 = jnp.max(mr2, axis=0, keepdims=True)
        qb1, qb2 = split(q1_ref[0])

        def fill_e(j, carry):
            l1, l2, mr1, mr2 = carry
            k0 = pl.multiple_of(j * kc, kc)
            l1 = expo(k0, s1_ref, p1_ref, ma1, l1)
            mr1 = scores(k0, qb1, s1_ref, mr1)
            l2 = expo(k0, s2_ref, p2_ref, ma2, l2)
            mr2 = scores(k0, qb2, s2_ref, mr2)
            return l1, l2, mr1, mr2

        l1, l2, mr1, mr2 = fill_loop(fill_e, (part(0.0), part(0.0), part(NEG_INF), part(NEG_INF)))
        st_ref[rows(0), :] = mr1
        st_ref[rows(1), :] = mr2
        st_ref[rows(2), :] = l1
        st_ref[rows(3), :] = l2

    m1, m2 = colmax(0), colmax(1)
    l1, l2 = colsum(2), colsum(3)
    c16 = jnp.broadcast_to((lam * l1 / l2).astype(BF16), (BF16_ROWS, tq))
    qs1, qs2 = split(q2_ref[0])

    def body(j, carry):
        acc, e1, e2, mr1, mr2 = carry
        k0 = pl.multiple_of(j * kc, kc)
        p2 = p2_ref[pl.ds(k0, kc), :].reshape(kc // BF16_ROWS, BF16_ROWS, tq)
        w = p1_ref[pl.ds(k0, kc), :] - (p2 * c16).reshape(kc, tq)
        acc = acc + _dot(vt_ref[0, :, pl.ds(k0, kc)], w)
        e1 = expo(k0, s1_ref, p1_ref, m1, e1)
        mr1 = scores(k0, qs1, s1_ref, mr1)
        e2 = expo(k0, s2_ref, p2_ref, m2, e2)
        mr2 = scores(k0, qs2, s2_ref, mr2)
        return acc, e1, e2, mr1, mr2

    acc, e1, e2, mr1, mr2 = loop(body, (jnp.zeros((2 * HEAD_DIM, tq), F32), part(0.0), part(0.0),
                                        part(NEG_INF), part(NEG_INF)))
    st_ref[rows(0), :] = mr1
    st_ref[rows(1), :] = mr2
    st_ref[rows(2), :] = e1
    st_ref[rows(3), :] = e2
    o = acc / l1
    ms = jnp.mean(o * o, axis=0, keepdims=True)
    y = o * lax.rsqrt(ms + EPS) * g_ref[...] * (1.0 - lam_init)
    o_ref[...] = y.T.astype(o_ref.dtype)


def _diff_attn(lam_p, g_col, qct, kc_nat, vct, batch, seq, layer, tq, kc):
    t = kc_nat.shape[0]
    lam_init = 0.8 - 0.6 * math.exp(-0.3 * layer)
    nq = seq // tq
    kern = functools.partial(_diff_kernel, seq=seq, tq=tq, kc=kc, unroll=min(16, seq // kc), lam_init=lam_init)
    total = batch * N_HEADS * nq
    assert nq >= 3, (seq, tq)

    def at(g, ahead):
        gg = jnp.minimum(g + ahead, total - 1)
        return gg // (N_HEADS * nq), (gg // nq) % N_HEADS, gg % nq

    def q_spec(ahead):
        return pl.BlockSpec((1, 2 * HEAD_DIM, tq), lambda g: at(g, ahead))

    def k_map(g):
        b, h, _ = at(g, 2)
        return b, h

    def v_map(g):
        b, h, _ = at(g, 0)
        return b, h, 0

    def o_map(g):
        b, h, i = at(g, 0)
        return b * nq + i, h

    return pl.pallas_call(
        kern,
        grid=(total,),
        in_specs=[pl.BlockSpec(lam_p.shape, lambda g: (0, 0)),
                  pl.BlockSpec((2 * HEAD_DIM, 1), lambda g: (0, 0)),
                  q_spec(0), q_spec(1), q_spec(2),
                  pl.BlockSpec((seq, 2 * HEAD_DIM), k_map),
                  pl.BlockSpec((1, 2 * HEAD_DIM, seq), v_map)],
        out_specs=pl.BlockSpec((tq, 2 * HEAD_DIM), o_map),
        out_shape=jax.ShapeDtypeStruct((t, C_WIDTH), BF16),
        scratch_shapes=[pltpu.VMEM((seq, tq), F32), pltpu.VMEM((seq, tq), F32),
                        pltpu.VMEM((seq, tq), BF16), pltpu.VMEM((seq, tq), BF16),
                        pltpu.VMEM((32, tq), F32)],
        compiler_params=_cparams(("arbitrary",)),
        name="diff_attn",
    )(lam_p, g_col, qct, qct, qct, kc_nat, vct)


def _out_proj_kernel(x_ref, o1_ref, l1_ref, o2_ref, l2_ref, o3_ref, l3_ref,
                     of_ref, ob_ref, gb_ref, oc_ref, hg_ref, w_ref, g_ref, h_ref, *stage_refs):
    tm = x_ref.shape[0]
    ab_width = A_WIDTH + B_WIDTH
    mix_c = _dot(oc_ref[...], w_ref[ab_width:, :])

    def natural(ref, dil, stage_ref):
        halves = range(A_WIDTH // LANES)
        for r in range(dil):
            blk = ref[0, r]
            for h in halves:
                stage_ref[h, pl.ds(r, tm // dil, stride=dil), :] = blk[:, h * LANES:(h + 1) * LANES]
        return jnp.concatenate([stage_ref[h] for h in halves], axis=1)

    o2, l2 = natural(o2_ref, DILATIONS[1], stage_refs[0]), natural(l2_ref, DILATIONS[1], stage_refs[1])
    o3, l3 = natural(o3_ref, DILATIONS[2], stage_refs[2]), natural(l3_ref, DILATIONS[2], stage_refs[3])
    l1 = l1_ref[...]
    mx = jnp.maximum(jnp.maximum(l1, l2), l3)
    w1, w2, w3 = jnp.exp(l1 - mx), jnp.exp(l2 - mx), jnp.exp(l3 - mx)
    oa = (w1 * o1_ref[...] + w2 * o2 + w3 * o3) / (w1 + w2 + w3)

    y = of_ref[...] + ob_ref[...]
    y2 = y * y
    lane = lax.broadcasted_iota(jnp.int32, y.shape, 1)
    ms = jnp.zeros_like(y)
    for h in range(N_HEADS):
        in_head = (lane >= h * HEAD_DIM) & (lane < (h + 1) * HEAD_DIM)
        ms_h = jnp.sum(jnp.where(in_head, y2, 0.0), axis=-1, keepdims=True) * (1.0 / HEAD_DIM)
        ms = jnp.where(in_head, ms_h, ms)
    gb = gb_ref[...]
    ob = y * lax.rsqrt(ms + EPS) * hg_ref[...] * (gb * jax.nn.sigmoid(gb))

    cat = jnp.concatenate([oa.astype(BF16), ob.astype(BF16)], axis=1)
    mix = _dot(cat, w_ref[0:ab_width, :]) + mix_c
    h_ref[...] = x_ref[...] + _rms(mix, g_ref[...])


def _out_proj(x2, a_outs, of, ob, gb, oc, hg, w, g, seq, tm):
    t, d = x2.shape
    tiles_per_seq = seq // tm
    row = lambda i: (i, 0)
    const = lambda i: (0, 0)
    r256 = pl.BlockSpec((tm, 256), row)
    sub = lambda dil: pl.BlockSpec((1, dil, tm // dil, A_WIDTH),
                                   lambda i: (i // tiles_per_seq, 0, i % tiles_per_seq, 0))
    a_specs = [r256, r256] + [sub(DILATIONS[1])] * 2 + [sub(DILATIONS[2])] * 2
    return pl.pallas_call(
        _out_proj_kernel,
        grid=(t // tm,),
        in_specs=[pl.BlockSpec((tm, d), row)] + a_specs + [r256] * 3 + [
            pl.BlockSpec((tm, C_WIDTH), row), pl.BlockSpec((1, B_WIDTH), const),
            pl.BlockSpec(w.shape, const), pl.BlockSpec((1, d), const)],
        out_specs=pl.BlockSpec((tm, d), row),
        out_shape=jax.ShapeDtypeStruct((t, d), F32),
        scratch_shapes=[pltpu.VMEM((A_WIDTH // LANES, tm, LANES), F32)] * 4,
        compiler_params=_cparams(("parallel",)),
        name="out_proj",
    )(x2, *a_outs, of, ob, gb, oc, hg, w, g)


def _ffn_kernel(hp_ref, h_ref, hn_ref, gpre_ref, wup_ref, cw_ref, cb_ref, wdn_ref, gpost_ref,
                o_ref, xe_ref, act_ref, *, tm, tiles_per_seq, d_ff, cn):
    i = pl.program_id(0)
    halo = BF16_ROWS
    first = (i % tiles_per_seq) == 0
    last = (i % tiles_per_seq) == tiles_per_seq - 1
    gpre = gpre_ref[...]
    h = h_ref[...]
    xe_ref[halo:halo + tm, :] = _rms(h, gpre).astype(BF16)
    xe_ref[0:halo, :] = jnp.where(first, 0.0, _rms(hp_ref[...], gpre)).astype(BF16)
    xe_ref[halo + tm:2 * halo + tm, :] = jnp.where(last, 0.0, _rms(hn_ref[...], gpre)).astype(BF16)
    xe = xe_ref[...]

    def conv(c0):
        u = _dot(xe, wup_ref[:, c0:c0 + cn])
        cw = cw_ref[:, c0:c0 + cn]
        return (u[halo - 1:halo - 1 + tm] * cw[0:1] + u[halo:halo + tm] * cw[1:2]
                + u[halo + 1:halo + 1 + tm] * cw[2:3] + cb_ref[:, c0:c0 + cn])

    for c in range(d_ff // cn):
        gate = conv(c * cn)
        val = conv(d_ff + c * cn)
        act_ref[:, c * cn:(c + 1) * cn] = (gate * jax.nn.sigmoid(gate) * val).astype(BF16)
    ff = _dot(act_ref[...], wdn_ref[...])
    o_ref[...] = h + _rms(ff, gpost_ref[...])


def _ffn(h2, gpre, wup, cw, cb, wdn, gpost, seq, tm, cn):
    t, d = h2.shape
    d_ff = wdn.shape[0]
    halo = BF16_ROWS
    tiles_per_seq = seq // tm
    hb = tm // halo
    const = lambda i: (0, 0)
    kern = functools.partial(_ffn_kernel, tm=tm, tiles_per_seq=tiles_per_seq, d_ff=d_ff, cn=cn)
    return pl.pallas_call(
        kern,
        grid=(t // tm,),
        in_specs=[pl.BlockSpec((halo, d), lambda i: (jnp.maximum(i * hb - 1, 0), 0)),
                  pl.BlockSpec((tm, d), lambda i: (i, 0)),
                  pl.BlockSpec((halo, d), lambda i: (jnp.minimum((i + 1) * hb, t // halo - 1), 0)),
                  pl.BlockSpec((1, d), const),
                  pl.BlockSpec(wup.shape, const), pl.BlockSpec(cw.shape, const),
                  pl.BlockSpec(cb.shape, const), pl.BlockSpec(wdn.shape, const),
                  pl.BlockSpec((1, d), const)],
        out_specs=pl.BlockSpec((tm, d), lambda i: (i, 0)),
        out_shape=jax.ShapeDtypeStruct((t, d), F32),
        scratch_shapes=[pltpu.VMEM((tm + 2 * halo, d), BF16), pltpu.VMEM((tm, d_ff), BF16)],
        compiler_params=_cparams(("parallel",)),
        name="conv_ffn",
    )(h2, h2, h2, gpre, wup, cw, cb, wdn, gpost)


def _rope_tables(seq):
    pos = jnp.arange(seq, dtype=F32)
    inv = ROPE_THETA ** (-jnp.arange(0, ROPE_DIM, 2, dtype=F32) / ROPE_DIM)
    ang = pos[:, None] * inv[None, :]
    cos, sin = jnp.cos(ang), jnp.sin(ang)
    rest = HEAD_DIM - ROPE_DIM
    one, zero = jnp.ones((seq, rest), F32), jnp.zeros((seq, rest), F32)
    zh = jnp.zeros((seq, ROPE_HALF), F32)
    rep = LANES // HEAD_DIM
    rc = jnp.tile(jnp.concatenate([cos, cos, one], axis=1), (1, rep))
    rs1 = jnp.tile(jnp.concatenate([-sin, zh, zero], axis=1), (1, rep))
    rs2 = jnp.tile(jnp.concatenate([zh, sin, zero], axis=1), (1, rep))
    return rc, rs1, rs2


class _Tiles(NamedTuple):
    rows: int
    q_tile: int
    key_chunk: int
    ff_chunk: int


def _tiles(seq):
    assert seq % (DILATIONS[-1] * BF16_ROWS) == 0 and seq % HGRN_CHUNK == 0, seq
    rows = min(2 * MXU_TILE, seq)
    assert seq % rows == 0 and rows % (DILATIONS[-1] * BF16_ROWS) == 0, (seq, rows)
    return _Tiles(rows=rows, q_tile=min(MXU_TILE, seq), key_chunk=min(2 * MXU_TILE, seq), ff_chunk=MXU_TILE)


def kernel(x, w_in, w_out, lb_logits, hgrn_norm, diff_lambda, diff_norm, w_up, conv_w, conv_b,
           w_down, norm_pre_mix, norm_post_mix, norm_pre_ffn, norm_post_ffn):
    batch, seq, d = x.shape
    depth = w_in.shape[0]
    tiles = _tiles(seq)
    tm = tiles.rows
    rc, rs1, rs2 = _rope_tables(seq)
    lbl2 = lb_logits.astype(F32).reshape(2 * depth, B_WIDTH)
    x2 = x.reshape(batch * seq, d)
    for l in range(depth):
        (qa, ka, va, qa4, ka4, va4, qa16, ka16, va16, qb, zf, zb, ib, gb, qct, kc, vct) = _in_proj(
            x2, norm_pre_mix[l].reshape(1, d), w_in[l].astype(BF16), rc, rs1, rs2, batch, seq, tm)

        a_outs = []
        for dil, qkv in zip(DILATIONS, ((qa, ka, va), (qa4, ka4, va4), (qa16, ka16, va16))):
            o, lse = _banded(*(a.reshape(batch * dil, seq // dil, A_WIDTH) for a in qkv))
            shape = (batch * seq, A_WIDTH) if dil == 1 else (batch, dil, seq // dil, A_WIDTH)
            a_outs += [o.reshape(shape), lse.reshape(shape)]

        of, ob = _hgrn(lbl2, qb, zf, zb, ib, batch, seq, l, rows=tm)

        oc = _diff_attn(diff_lambda[l].astype(F32), diff_norm[l].astype(F32).reshape(2 * HEAD_DIM, 1),
                        qct, kc, vct, batch, seq, l, tq=tiles.q_tile, kc=tiles.key_chunk)

        hg = jnp.tile(hgrn_norm[l].astype(F32), N_HEADS).reshape(1, B_WIDTH)
        h2 = _out_proj(x2, a_outs, of, ob, gb, oc, hg, w_out[l].astype(BF16),
                       norm_post_mix[l].reshape(1, d), seq, tm)

        x2 = _ffn(h2, norm_pre_ffn[l].reshape(1, d), w_up[l].astype(BF16), conv_w[l],
                  conv_b[l].reshape(1, -1), w_down[l].astype(BF16), norm_post_ffn[l].reshape(1, d),
                  seq, tm, cn=tiles.ff_chunk)
    return x2.reshape(batch, seq, d)
```

```python
import functools
import math
from typing import NamedTuple

import jax
import jax.numpy as jnp
from jax import lax
from jax.experimental import pallas as pl
from jax.experimental.pallas import tpu as pltpu

F32 = jnp.float32
BF16 = jnp.bfloat16

HEAD_DIM = 64
N_HEADS = 4
A_WIDTH = N_HEADS * HEAD_DIM
B_WIDTH = N_HEADS * HEAD_DIM
C_WIDTH = N_HEADS * 2 * HEAD_DIM
ROPE_THETA = 500000.0
ROPE_DIM = HEAD_DIM // 4
ROPE_HALF = ROPE_DIM // 2
DILATIONS = (1, 4, 16)
BAND_HALF = 64
HGRN_CHUNK = 64
CONV_WIDTH = 3
EPS = 1e-6
NEG_INF = -1e30
LOG2E = math.log2(math.e)
LN2 = math.log(2.0)
EXP2_CLAMP = 115.0

LANES = 128
BF16_ROWS = 16
MXU_TILE = 256
VMEM_LIMIT = 56 * 1024 * 1024

OFF_QA, OFF_KA, OFF_VA = 0, 256, 512
OFF_QB, OFF_ZF, OFF_ZB, OFF_IB, OFF_GB = 768, 1024, 1280, 1536, 1792
OFF_QC, OFF_KC, OFF_VC = 2048, 2560, 3072


def _cparams(sem):
    return pltpu.CompilerParams(dimension_semantics=sem, vmem_limit_bytes=VMEM_LIMIT)


def _rms(x, g):
    ms = jnp.mean(x * x, axis=-1, keepdims=True)
    return x * lax.rsqrt(ms + EPS) * g


def _dot(a, b):
    return jnp.dot(a, b, preferred_element_type=F32)


def _dot_nt(a, b):
    return lax.dot_general(a, b, (((1,), (1,)), ((), ())), preferred_element_type=F32)


def _dot_tn(a, b):
    return lax.dot_general(a, b, (((0,), (0,)), ((), ())), preferred_element_type=F32)


def _in_proj_kernel(x_ref, g_ref, w_ref, rc_ref, rs1_ref, rs2_ref,
                    qa_ref, ka_ref, va_ref, qa4_ref, ka4_ref, va4_ref, qa16_ref, ka16_ref, va16_ref,
                    qb_ref, zf_ref, zb_ref, ib_ref, gb_ref, qct_ref, kc_ref, vct_ref,
                    sq_ref, sk_ref, sv_ref):
    tm = x_ref.shape[0]
    xn = _rms(x_ref[...], g_ref[...]).astype(BF16)

    def emit_a(val, stage_ref, nat_ref, sub_refs):
        nat_ref[...] = val.astype(BF16)
        halves = range(A_WIDTH // LANES)
        for h in halves:
            stage_ref[h] = val[:, h * LANES:(h + 1) * LANES]
        for dil, ref in zip(DILATIONS[1:], sub_refs):
            for r in range(dil):
                ref[0, r, :, :] = jnp.concatenate(
                    [stage_ref[h, pl.ds(r, tm // dil, stride=dil), :] for h in halves], axis=1).astype(BF16)

    rc, rs1, rs2 = rc_ref[...], rs1_ref[...], rs2_ref[...]

    def proj(c0, n):
        return _dot(xn, w_ref[:, c0:c0 + n])

    def rope(a):
        outs = []
        for j in range(a.shape[1] // LANES):
            blk = a[:, j * LANES:(j + 1) * LANES]
            outs.append(blk * rc + pltpu.roll(blk, LANES - ROPE_HALF, 1) * rs1
                        + pltpu.roll(blk, ROPE_HALF, 1) * rs2)
        return outs

    def cat(blks):
        return jnp.concatenate(blks, axis=1)

    scale = 1.0 / math.sqrt(HEAD_DIM)
    for half in range(2):
        qblks = rope(proj(OFF_QC + 256 * half, 256))
        kblks = rope(proj(OFF_KC + 256 * half, 256))
        v = proj(OFF_VC + 256 * half, 256)
        kc_ref[:, 256 * half:256 * (half + 1)] = cat(kblks).astype(BF16)
        for j in range(2):
            r0 = 256 * half + LANES * j
            qct_ref[0, r0:r0 + LANES, :] = (qblks[j] * (scale * LOG2E)).T.astype(BF16)
            vct_ref[0, r0:r0 + LANES, :] = v[:, j * LANES:(j + 1) * LANES].T.astype(BF16)
    emit_a(cat(rope(proj(OFF_QA, 256))) * (scale * LOG2E), sq_ref, qa_ref, (qa4_ref, qa16_ref))
    emit_a(cat(rope(proj(OFF_KA, 256))), sk_ref, ka_ref, (ka4_ref, ka16_ref))
    emit_a(proj(OFF_VA, 256), sv_ref, va_ref, (va4_ref, va16_ref))
    qb_ref[...] = proj(OFF_QB, 256).astype(BF16)
    zf_ref[...] = proj(OFF_ZF, 256)
    zb_ref[...] = proj(OFF_ZB, 256)
    ib_ref[...] = proj(OFF_IB, 256).astype(BF16)
    gb_ref[...] = proj(OFF_GB, 256)


def _in_proj(x2, g, w, rc, rs1, rs2, batch, seq, tm):
    t, d = x2.shape
    n_cols = w.shape[1]
    tiles_per_seq = seq // tm
    row = lambda i: (i, 0)
    const = lambda i: (0, 0)
    ropei = lambda i: (i % tiles_per_seq, 0)
    tr = lambda i: (i // tiles_per_seq, 0, i % tiles_per_seq)
    nat = lambda n, dt: jax.ShapeDtypeStruct((t, n), dt)
    sub_shape = lambda dil: jax.ShapeDtypeStruct((batch, dil, seq // dil, A_WIDTH), BF16)
    sub_spec = lambda dil: pl.BlockSpec((1, dil, tm // dil, A_WIDTH),
                                        lambda i: (i // tiles_per_seq, 0, i % tiles_per_seq, 0))
    out_shape = ([nat(256, BF16)] * 3
                 + [sub_shape(DILATIONS[1])] * 3 + [sub_shape(DILATIONS[2])] * 3
                 + [nat(256, BF16), nat(256, F32), nat(256, F32), nat(256, BF16), nat(256, F32)]
                 + [jax.ShapeDtypeStruct((batch, C_WIDTH, seq), BF16),
                    nat(C_WIDTH, BF16),
                    jax.ShapeDtypeStruct((batch, C_WIDTH, seq), BF16)])
    out_specs = ([pl.BlockSpec((tm, 256), row)] * 3 + [sub_spec(DILATIONS[1])] * 3 + [sub_spec(DILATIONS[2])] * 3
                 + [pl.BlockSpec((tm, 256), row)] * 5
                 + [pl.BlockSpec((1, C_WIDTH, tm), tr), pl.BlockSpec((tm, C_WIDTH), row),
                    pl.BlockSpec((1, C_WIDTH, tm), tr)])
    return pl.pallas_call(
        _in_proj_kernel,
        grid=(t // tm,),
        in_specs=[pl.BlockSpec((tm, d), row), pl.BlockSpec((1, d), const),
                  pl.BlockSpec((d, n_cols), const),
                  pl.BlockSpec((tm, LANES), ropei), pl.BlockSpec((tm, LANES), ropei),
                  pl.BlockSpec((tm, LANES), ropei)],
        out_specs=out_specs,
        out_shape=out_shape,
        scratch_shapes=[pltpu.VMEM((A_WIDTH // LANES, tm, LANES), F32)] * 3,
        compiler_params=_cparams(("parallel",)),
        name="in_proj",
    )(x2, g, w, rc, rs1, rs2)


def _banded_kernel(q_ref, k_ref, v_ref, o_ref, lse_ref, *, seq_len, q_block, sub, k_win):
    i = pl.program_id(1)
    rows = N_HEADS * sub
    lane = lax.broadcasted_iota(jnp.int32, (rows, A_WIDTH), 1)
    row_head = lax.broadcasted_iota(jnp.int32, (rows, A_WIDTH), 0) // sub
    own_lanes = (lane // HEAD_DIM) == row_head
    rel = (lax.broadcasted_iota(jnp.int32, (rows, k_win), 1)
           - lax.broadcasted_iota(jnp.int32, (rows, k_win), 0) % sub)
    head_lane = lax.broadcasted_iota(jnp.int32, (sub, A_WIDTH), 1) // HEAD_DIM
    n_sub = q_block // sub
    q0s = [i * q_block + sb * sub for sb in range(n_sub)]
    kss = [pl.multiple_of(jnp.clip(q0 - BAND_HALF, 0, seq_len - k_win), BAND_HALF) for q0 in q0s]

    def score(sb):
        qblk = q_ref[0, sb * sub:(sb + 1) * sub, :]
        q4 = jnp.concatenate([qblk] * N_HEADS, axis=0)
        q4 = jnp.where(own_lanes, q4, jnp.zeros_like(q4))
        return _dot_nt(q4, k_ref[0, pl.ds(kss[sb], k_win), :])

    def softmax(sb, s):
        shifted = rel + (kss[sb] - q0s[sb] + BAND_HALF)
        band = shifted.astype(jnp.uint32) <= 2 * BAND_HALF
        s = jnp.where(band, s, NEG_INF)
        m = jnp.max(s, axis=-1, keepdims=True)
        p = jnp.exp2(s - m)
        den = jnp.sum(p, axis=-1, keepdims=True)
        return p.astype(BF16), den, (m + jnp.log2(den)) * LN2

    def per_head(x4):
        out = x4[(N_HEADS - 1) * sub:N_HEADS * sub]
        for h in reversed(range(N_HEADS - 1)):
            out = jnp.where(head_lane == h, x4[h * sub:(h + 1) * sub], out)
        return out

    ss = [score(sb) for sb in range(n_sub)]
    sm = [softmax(sb, s) for sb, s in enumerate(ss)]
    pv = [_dot(p, v_ref[0, pl.ds(kss[sb], k_win), :]) for sb, (p, _, _) in enumerate(sm)]
    for sb in range(n_sub):
        _, den, lse = sm[sb]
        o_ref[0, sb * sub:(sb + 1) * sub, :] = per_head(pv[sb]) / per_head(den)
        lse_ref[0, sb * sub:(sb + 1) * sub, :] = per_head(lse)


def _banded(q, k, v):
    n_seq, seq_len, w = q.shape
    q_block = min(2 * MXU_TILE, seq_len)
    sub = min(LANES, seq_len)
    k_win = min(sub + 2 * BAND_HALF, seq_len)
    kern = functools.partial(_banded_kernel, seq_len=seq_len, q_block=q_block, sub=sub, k_win=k_win)
    full = pl.BlockSpec((1, seq_len, w), lambda s, i: (s, 0, 0))
    blk = pl.BlockSpec((1, q_block, w), lambda s, i: (s, i, 0))
    return pl.pallas_call(
        kern,
        grid=(n_seq, seq_len // q_block),
        in_specs=[blk, full, full],
        out_specs=[blk, blk],
        out_shape=[jax.ShapeDtypeStruct((n_seq, seq_len, w), F32)] * 2,
        compiler_params=_cparams(("parallel", "arbitrary")),
        name="banded_attn",
    )(q, k, v)


def _split3(x):
    hi = x.astype(BF16)
    r1 = x - hi.astype(F32)
    mid = r1.astype(BF16)
    lo = (r1 - mid.astype(F32)).astype(BF16)
    return hi, mid, lo


def _hgrn_kernel(lbl_ref, lbd_ref, qf_ref, zf_ref, vf_ref, qb_ref, zb_ref, vb_ref,
                 of_ref, ob_ref, sf_ref, sb_ref, *, depth, layer, n_chunks):
    c = HGRN_CHUNK
    rows = n_chunks * c
    group = lbd_ref.shape[0]

    @pl.when(pl.program_id(2) == 0)
    def _():
        sf_ref[...] = jnp.zeros_like(sf_ref)
        sb_ref[...] = jnp.zeros_like(sb_ref)

    def lower_bound(logits):
        e = jnp.exp(logits - jnp.max(logits, axis=0, keepdims=True))
        p = e / jnp.sum(e, axis=0, keepdims=True)
        lb = jnp.zeros((1, LANES), F32)
        for j in range(1, layer + 1):
            lb = lb + p[j:j + 1, :]
        return lb

    lb_f = lower_bound(lbl_ref[0:depth, :])
    lb_b = lower_bound(lbl_ref[depth:2 * depth, :])

    ti = lax.broadcasted_iota(jnp.int32, (c, c), 0)
    si = lax.broadcasted_iota(jnp.int32, (c, c), 1)
    tril = si <= ti
    triu = si >= ti
    lane_c = lax.broadcasted_iota(jnp.int32, (c, LANES), 1) < HEAD_DIM
    lane_b = lax.broadcasted_iota(jnp.int32, (rows, LANES), 1) < HEAD_DIM
    er = lax.broadcasted_iota(jnp.int32, (LANES, LANES), 0) < HEAD_DIM
    ec = lax.broadcasted_iota(jnp.int32, (LANES, LANES), 1) < HEAD_DIM
    same_head = er == ec
    lbd = lbd_ref[...]

    def cumsum(logf):
        parts = jnp.concatenate(_split3(logf), axis=1)
        outs = []
        for g in range(rows // group):
            r = _dot(lbd, parts[g * group:(g + 1) * group, :])
            outs.append(r[:, 0:LANES] + r[:, LANES:2 * LANES] + r[:, 2 * LANES:3 * LANES])
        return jnp.concatenate(outs, axis=0)

    def chunk_rows(x, off):
        return jnp.concatenate(
            [jnp.broadcast_to(x[j * c + off:j * c + off + 1, :], (c, LANES)) for j in range(n_chunks)], axis=0)

    rs = [slice(j * c, (j + 1) * c) for j in range(n_chunks)]

    def prepare(q_ref, z_ref, v_ref, lb, forward):
        q = q_ref[...].astype(F32)
        z = z_ref[...]
        v = v_ref[...]
        sig = jax.nn.sigmoid(z)
        logf = jnp.log2(lb + (1.0 - lb) * sig)
        kk = (1.0 - lb) * (1.0 - sig)
        a = cumsum(logf)
        last = chunk_rows(a, c - 1)
        if forward:
            e = a
            mid = chunk_rows(a, c // 2 - 1)
            q_in = q * jnp.exp2(a)
            k_st = kk * jnp.exp2(last - a)
            tri = tril
        else:
            e = a - logf
            mid = chunk_rows(e, c // 2)
            q_in = q * jnp.exp2(last - e)
            k_st = kk * jnp.exp2(e)
            tri = triu
        sgn = 1.0 if forward else -1.0
        qd = q * jnp.exp2(jnp.minimum(sgn * (e - mid), EXP2_CLAMP))
        kd = (kk * jnp.exp2(jnp.minimum(sgn * (mid - e), EXP2_CLAMP))).astype(BF16)
        q0 = jnp.where(lane_b, qd, 0.0).astype(BF16)
        q1 = jnp.where(lane_b, 0.0, qd).astype(BF16)
        decay = [jnp.exp2(a[(j + 1) * c - 1:(j + 1) * c, :]) for j in range(n_chunks)]
        return dict(q0=q0, q1=q1, kd=kd, v=v, q_in=q_in.astype(BF16), k_st=k_st.astype(BF16), tri=tri,
                    decay=decay, forward=forward)

    def first_dots(d):
        return ([_dot_nt(d["q0"][r], d["kd"][r]) for r in rs], [_dot_nt(d["q1"][r], d["kd"][r]) for r in rs],
                [_dot_tn(d["v"][r], d["k_st"][r]) for r in rs])

    def intra_dots(d, s0, s1):
        s0 = [jnp.where(d["tri"], s, 0.0).astype(BF16) for s in s0]
        s1 = [jnp.where(d["tri"], s, 0.0).astype(BF16) for s in s1]
        return [jnp.where(lane_c, _dot(x0, d["v"][r]), _dot(x1, d["v"][r])) for x0, x1, r in zip(s0, s1, rs)]

    def scan(d, ut, st_ref):
        st = st_ref[...]
        states = [None] * n_chunks
        for j in (range(n_chunks) if d["forward"] else reversed(range(n_chunks))):
            states[j] = st.astype(BF16)
            st = st * d["decay"][j] + jnp.where(same_head, ut[j], 0.0)
        st_ref[...] = st
        return states

    def finish(d, states, intra, o_ref):
        for j, r in enumerate(rs):
            o_ref[r, :] = _dot_nt(d["q_in"][r], states[j]) + intra[j]

    df = prepare(qf_ref, zf_ref, vf_ref, lb_f, True)
    db = prepare(qb_ref, zb_ref, vb_ref, lb_b, False)
    f0, f1, fu = first_dots(df)
    b0, b1, bu = first_dots(db)
    fi = intra_dots(df, f0, f1)
    bi = intra_dots(db, b0, b1)
    fs = scan(df, fu, sf_ref)
    bs = scan(db, bu, sb_ref)
    finish(df, fs, fi, of_ref)
    finish(db, bs, bi, ob_ref)


def _hgrn(lb_logits2, q, zf, zb, v, batch, seq, layer, rows):
    t = q.shape[0]
    depth = lb_logits2.shape[0] // 2
    n = seq // rows
    fwd = lambda b, p, i: (b * n + i, p)
    bwd = lambda b, p, i: (b * n + (n - 1 - i), p)
    kern = functools.partial(_hgrn_kernel, depth=depth, layer=layer, n_chunks=rows // HGRN_CHUNK)
    spec = lambda im: pl.BlockSpec((rows, LANES), im)
    group = min(256, rows)
    idx = jnp.arange(group)
    lbd = ((idx[None, :] <= idx[:, None])
           & (idx[None, :] // HGRN_CHUNK == idx[:, None] // HGRN_CHUNK)).astype(BF16)
    return pl.pallas_call(
        kern,
        grid=(batch, B_WIDTH // LANES, n),
        in_specs=[pl.BlockSpec((2 * depth, LANES), lambda b, p, i: (0, p)),
                  pl.BlockSpec((group, group), lambda b, p, i: (0, 0)),
                  spec(fwd), spec(fwd), spec(fwd), spec(bwd), spec(bwd), spec(bwd)],
        out_specs=[spec(fwd), spec(bwd)],
        out_shape=[jax.ShapeDtypeStruct((t, B_WIDTH), F32)] * 2,
        scratch_shapes=[pltpu.VMEM((LANES, LANES), F32), pltpu.VMEM((LANES, LANES), F32)],
        compiler_params=_cparams(("parallel", "parallel", "arbitrary")),
        name="hgrn2",
    )(lb_logits2, lbd, q, zf, v, q, zb, v)


def _diff_kernel(lam_ref, g_ref, q0_ref, q1_ref, q2_ref, k_ref, vt_ref, o_ref,
                 s1_ref, s2_ref, p1_ref, p2_ref, st_ref, *, seq, tq, kc, unroll, lam_init):
    lp = lam_ref[...]
    lam = (jnp.exp(jnp.sum(lp[0:1, :] * lp[1:2, :], axis=1, keepdims=True))
           - jnp.exp(jnp.sum(lp[2:3, :] * lp[3:4, :], axis=1, keepdims=True)) + lam_init)
    n = seq // kc
    sub = 8

    def split(qt):
        row = lax.broadcasted_iota(jnp.int32, qt.shape, 0)
        zero = jnp.zeros_like(qt)
        return jnp.where(row < HEAD_DIM, qt, zero), jnp.where(row < HEAD_DIM, zero, qt)

    def scores(k0, q, s_ref, mrun):
        s = _dot(k_ref[pl.ds(k0, kc), :], q)
        s_ref[pl.ds(k0, kc), :] = s
        return jnp.maximum(mrun, jnp.max(s.reshape(kc // sub, sub, tq), axis=0))

    def expo(k0, s_ref, p_ref, m, lrun):
        p = jnp.exp2(s_ref[pl.ds(k0, kc), :] - m)
        p_ref[pl.ds(k0, kc), :] = p.astype(BF16)
        return lrun + jnp.sum(p.reshape(kc // sub, sub, tq), axis=0)

    part = lambda val: jnp.full((sub, tq), val, F32)
    loop = functools.partial(lax.fori_loop, 0, n, unroll=unroll)
    fill_loop = functools.partial(lax.fori_loop, 0, n, unroll=min(4, n))
    rows = lambda a: slice(a * sub, (a + 1) * sub)
    colmax = lambda a: jnp.max(st_ref[rows(a), :], axis=0, keepdims=True)
    colsum = lambda a: jnp.sum(st_ref[rows(a), :], axis=0, keepdims=True)

    @pl.when(pl.program_id(0) == 0)
    def _():
        qa1, qa2 = split(q0_ref[0])

        def fill_s(j, carry):
            k0 = pl.multiple_of(j * kc, kc)
            return scores(k0, qa1, s1_ref, carry[0]), scores(k0, qa2, s2_ref, carry[1])

        mr1, mr2 = fill_loop(fill_s, (part(NEG_INF), part(NEG_INF)))
        ma1 = jnp.max(mr1, axis=0, keepdims=True)
        ma2 = jnp.max(mr2, axis=0, keepdims=True)
        qb1, qb2 = split(q1_ref[0])

        def fill_e(j, carry):
            l1, l2, mr1, mr2 = carry
            k0 = pl.multiple_of(j * kc, kc)
            l1 = expo(k0, s1_ref, p1_ref, ma1, l1)
            mr1 = scores(k0, qb1, s1_ref, mr1)
            l2 = expo(k0, s2_ref, p2_ref, ma2, l2)
            mr2 = scores(k0, qb2, s2_ref, mr2)
            return l1, l2, mr1, mr2

        l1, l2, mr1, mr2 = fill_loop(fill_e, (part(0.0), part(0.0), part(NEG_INF), part(NEG_INF)))
        st_ref[rows(0), :] = mr1
        st_ref[rows(1), :] = mr2
        st_ref[rows(2), :] = l1
        st_ref[rows(3), :] = l2

    m1, m2 = colmax(0), colmax(1)
    l1, l2 = colsum(2), colsum(3)
    c16 = jnp.broadcast_to((lam * l1 / l2).astype(BF16), (BF16_ROWS, tq))
    qs1, qs2 = split(q2_ref[0])

    def body(j, carry):
        acc, e1, e2, mr1, mr2 = carry
        k0 = pl.multiple_of(j * kc, kc)
        p2 = p2_ref[pl.ds(k0, kc), :].reshape(kc // BF16_ROWS, BF16_ROWS, tq)
        w = p1_ref[pl.ds(k0, kc), :] - (p2 * c16).reshape(kc, tq)
        acc = acc + _dot(vt_ref[0, :, pl.ds(k0, kc)], w)
        e1 = expo(k0, s1_ref, p1_ref, m1, e1)
        mr1 = scores(k0, qs1, s1_ref, mr1)
        e2 = expo(k0, s2_ref, p2_ref, m2, e2)
        mr2 = scores(k0, qs2, s2_ref, mr2)
        return acc, e1, e2, mr1, mr2

    acc, e1, e2, mr1, mr2 = loop(body, (jnp.zeros((2 * HEAD_DIM, tq), F32), part(0.0), part(0.0),
                                        part(NEG_INF), part(NEG_INF)))
    st_ref[rows(0), :] = mr1
    st_ref[rows(1), :] = mr2
    st_ref[rows(2), :] = e1
    st_ref[rows(3), :] = e2
    o = acc / l1
    ms = jnp.mean(o * o, axis=0, keepdims=True)
    y = o * lax.rsqrt(ms + EPS) * g_ref[...] * (1.0 - lam_init)
    o_ref[...] = y.T.astype(o_ref.dtype)


def _diff_attn(lam_p, g_col, qct, kc_nat, vct, batch, seq, layer, tq, kc):
    t = kc_nat.shape[0]
    lam_init = 0.8 - 0.6 * math.exp(-0.3 * layer)
    nq = seq // tq
    kern = functools.partial(_diff_kernel, seq=seq, tq=tq, kc=kc, unroll=min(16, seq // kc), lam_init=lam_init)
    total = batch * N_HEADS * nq
    assert nq >= 3, (seq, tq)

    def at(g, ahead):
        gg = jnp.minimum(g + ahead, total - 1)
        return gg // (N_HEADS * nq), (gg // nq) % N_HEADS, gg % nq

    def q_spec(ahead):
        return pl.BlockSpec((1, 2 * HEAD_DIM, tq), lambda g: at(g, ahead))

    def k_map(g):
        b, h, _ = at(g, 2)
        return b, h

    def v_map(g):
        b, h, _ = at(g, 0)
        return b, h, 0

    def o_map(g):
        b, h, i = at(g, 0)
        return b * nq + i, h

    return pl.pallas_call(
        kern,
        grid=(total,),
        in_specs=[pl.BlockSpec(lam_p.shape, lambda g: (0, 0)),
                  pl.BlockSpec((2 * HEAD_DIM, 1), lambda g: (0, 0)),
                  q_spec(0), q_spec(1), q_spec(2),
                  pl.BlockSpec((seq, 2 * HEAD_DIM), k_map),
                  pl.BlockSpec((1, 2 * HEAD_DIM, seq), v_map)],
        out_specs=pl.BlockSpec((tq, 2 * HEAD_DIM), o_map),
        out_shape=jax.ShapeDtypeStruct((t, C_WIDTH), BF16),
        scratch_shapes=[pltpu.VMEM((seq, tq), F32), pltpu.VMEM((seq, tq), F32),
                        pltpu.VMEM((seq, tq), BF16), pltpu.VMEM((seq, tq), BF16),
                        pltpu.VMEM((32, tq), F32)],
        compiler_params=_cparams(("arbitrary",)),
        name="diff_attn",
    )(lam_p, g_col, qct, qct, qct, kc_nat, vct)


def _out_proj_kernel(x_ref, o1_ref, l1_ref, o2_ref, l2_ref, o3_ref, l3_ref,
                     of_ref, ob_ref, gb_ref, oc_ref, hg_ref, w_ref, g_ref, h_ref, *stage_refs):
    tm = x_ref.shape[0]
    ab_width = A_WIDTH + B_WIDTH
    mix_c = _dot(oc_ref[...], w_ref[ab_width:, :])

    def natural(ref, dil, stage_ref):
        halves = range(A_WIDTH // LANES)
        for r in range(dil):
            blk = ref[0, r]
            for h in halves:
                stage_ref[h, pl.ds(r, tm // dil, stride=dil), :] = blk[:, h * LANES:(h + 1) * LANES]
        return jnp.concatenate([stage_ref[h] for h in halves], axis=1)

    o2, l2 = natural(o2_ref, DILATIONS[1], stage_refs[0]), natural(l2_ref, DILATIONS[1], stage_refs[1])
    o3, l3 = natural(o3_ref, DILATIONS[2], stage_refs[2]), natural(l3_ref, DILATIONS[2], stage_refs[3])
    l1 = l1_ref[...]
    mx = jnp.maximum(jnp.maximum(l1, l2), l3)
    w1, w2, w3 = jnp.exp(l1 - mx), jnp.exp(l2 - mx), jnp.exp(l3 - mx)
    oa = (w1 * o1_ref[...] + w2 * o2 + w3 * o3) / (w1 + w2 + w3)

    y = of_ref[...] + ob_ref[...]
    y2 = y * y
    lane = lax.broadcasted_iota(jnp.int32, y.shape, 1)
    ms = jnp.zeros_like(y)
    for h in range(N_HEADS):
        in_head = (lane >= h * HEAD_DIM) & (lane < (h + 1) * HEAD_DIM)
        ms_h = jnp.sum(jnp.where(in_head, y2, 0.0), axis=-1, keepdims=True) * (1.0 / HEAD_DIM)
        ms = jnp.where(in_head, ms_h, ms)
    gb = gb_ref[...]
    ob = y * lax.rsqrt(ms + EPS) * hg_ref[...] * (gb * jax.nn.sigmoid(gb))

    cat = jnp.concatenate([oa.astype(BF16), ob.astype(BF16)], axis=1)
    mix = _dot(cat, w_ref[0:ab_width, :]) + mix_c
    h_ref[...] = x_ref[...] + _rms(mix, g_ref[...])


def _out_proj(x2, a_outs, of, ob, gb, oc, hg, w, g, seq, tm):
    t, d = x2.shape
    tiles_per_seq = seq // tm
    row = lambda i: (i, 0)
    const = lambda i: (0, 0)
    r256 = pl.BlockSpec((tm, 256), row)
    sub = lambda dil: pl.BlockSpec((1, dil, tm // dil, A_WIDTH),
                                   lambda i: (i // tiles_per_seq, 0, i % tiles_per_seq, 0))
    a_specs = [r256, r256] + [sub(DILATIONS[1])] * 2 + [sub(DILATIONS[2])] * 2
    return pl.pallas_call(
        _out_proj_kernel,
        grid=(t // tm,),
        in_specs=[pl.BlockSpec((tm, d), row)] + a_specs + [r256] * 3 + [
            pl.BlockSpec((tm, C_WIDTH), row), pl.BlockSpec((1, B_WIDTH), const),
            pl.BlockSpec(w.shape, const), pl.BlockSpec((1, d), const)],
        out_specs=pl.BlockSpec((tm, d), row),
        out_shape=jax.ShapeDtypeStruct((t, d), F32),
        scratch_shapes=[pltpu.VMEM((A_WIDTH // LANES, tm, LANES), F32)] * 4,
        compiler_params=_cparams(("parallel",)),
        name="out_proj",
    )(x2, *a_outs, of, ob, gb, oc, hg, w, g)


def _ffn_kernel(hp_ref, h_ref, hn_ref, gpre_ref, wup_ref, cw_ref, cb_ref, wdn_ref, gpost_ref,
                o_ref, xe_ref, act_ref, *, tm, tiles_per_seq, d_ff, cn):
    i = pl.program_id(0)
    halo = BF16_ROWS
    first = (i % tiles_per_seq) == 0
    last = (i % tiles_per_seq) == tiles_per_seq - 1
    gpre = gpre_ref[...]
    h = h_ref[...]
    xe_ref[halo:halo + tm, :] = _rms(h, gpre).astype(BF16)
    xe_ref[0:halo, :] = jnp.where(first, 0.0, _rms(hp_ref[...], gpre)).astype(BF16)
    xe_ref[halo + tm:2 * halo + tm, :] = jnp.where(last, 0.0, _rms(hn_ref[...], gpre)).astype(BF16)
    xe = xe_ref[...]

    def conv(c0):
        u = _dot(xe, wup_ref[:, c0:c0 + cn])
        cw = cw_ref[:, c0:c0 + cn]
        return (u[halo - 1:halo - 1 + tm] * cw[0:1] + u[halo:halo + tm] * cw[1:2]
                + u[halo + 1:halo + 1 + tm] * cw[2:3] + cb_ref[:, c0:c0 + cn])

    for c in range(d_ff // cn):
        gate = conv(c * cn)
        val = conv(d_ff + c * cn)
        act_ref[:, c * cn:(c + 1) * cn] = (gate * jax.nn.sigmoid(gate) * val).astype(BF16)
    ff = _dot(act_ref[...], wdn_ref[...])
    o_ref[...] = h + _rms(ff, gpost_ref[...])


def _ffn(h2, gpre, wup, cw, cb, wdn, gpost, seq, tm, cn):
    t, d = h2.shape
    d_ff = wdn.shape[0]
    halo = BF16_ROWS
    tiles_per_seq = seq // tm
    hb = tm // halo
    const = lambda i: (0, 0)
    kern = functools.partial(_ffn_kernel, tm=tm, tiles_per_seq=tiles_per_seq, d_ff=d_ff, cn=cn)
    return pl.pallas_call(
        kern,
        grid=(t // tm,),
        in_specs=[pl.BlockSpec((halo, d), lambda i: (jnp.maximum(i * hb - 1, 0), 0)),
                  pl.BlockSpec((tm, d), lambda i: (i, 0)),
                  pl.BlockSpec((halo, d), lambda i: (jnp.minimum((i + 1) * hb, t // halo - 1), 0)),
                  pl.BlockSpec((1, d), const),
                  pl.BlockSpec(wup.shape, const), pl.BlockSpec(cw.shape, const),
                  pl.BlockSpec(cb.shape, const), pl.BlockSpec(wdn.shape, const),
                  pl.BlockSpec((1, d), const)],
        out_specs=pl.BlockSpec((tm, d), lambda i: (i, 0)),
        out_shape=jax.ShapeDtypeStruct((t, d), F32),
        scratch_shapes=[pltpu.VMEM((tm + 2 * halo, d), BF16), pltpu.VMEM((tm, d_ff), BF16)],
        compiler_params=_cparams(("parallel",)),
        name="conv_ffn",
    )(h2, h2, h2, gpre, wup, cw, cb, wdn, gpost)


def _rope_tables(seq):
    pos = jnp.arange(seq, dtype=F32)
    inv = ROPE_THETA ** (-jnp.arange(0, ROPE_DIM, 2, dtype=F32) / ROPE_DIM)
    ang = pos[:, None] * inv[None, :]
    cos, sin = jnp.cos(ang), jnp.sin(ang)
    rest = HEAD_DIM - ROPE_DIM
    one, zero = jnp.ones((seq, rest), F32), jnp.zeros((seq, rest), F32)
    zh = jnp.zeros((seq, ROPE_HALF), F32)
    rep = LANES // HEAD_DIM
    rc = jnp.tile(jnp.concatenate([cos, cos, one], axis=1), (1, rep))
    rs1 = jnp.tile(jnp.concatenate([-sin, zh, zero], axis=1), (1, rep))
    rs2 = jnp.tile(jnp.concatenate([zh, sin, zero], axis=1), (1, rep))
    return rc, rs1, rs2


class _Tiles(NamedTuple):
    rows: int
    q_tile: int
    key_chunk: int
    ff_chunk: int


def _tiles(seq):
    assert seq % (DILATIONS[-1] * BF16_ROWS) == 0 and seq % HGRN_CHUNK == 0, seq
    rows = min(2 * MXU_TILE, seq)
    assert seq % rows == 0 and rows % (DILATIONS[-1] * BF16_ROWS) == 0, (seq, rows)
    return _Tiles(rows=rows, q_tile=min(MXU_TILE, seq), key_chunk=min(2 * MXU_TILE, seq), ff_chunk=MXU_TILE)


def kernel(x, w_in, w_out, lb_logits, hgrn_norm, diff_lambda, diff_norm, w_up, conv_w, conv_b,
           w_down, norm_pre_mix, norm_post_mix, norm_pre_ffn, norm_post_ffn):
    batch, seq, d = x.shape
    depth = w_in.shape[0]
    tiles = _tiles(seq)
    tm = tiles.rows
    rc, rs1, rs2 = _rope_tables(seq)
    lbl2 = lb_logits.astype(F32).reshape(2 * depth, B_WIDTH)
    x2 = x.reshape(batch * seq, d)
    for l in range(depth):
        (qa, ka, va, qa4, ka4, va4, qa16, ka16, va16, qb, zf, zb, ib, gb, qct, kc, vct) = _in_proj(
            x2, norm_pre_mix[l].reshape(1, d), w_in[l].astype(BF16), rc, rs1, rs2, batch, seq, tm)

        a_outs = []
        for dil, qkv in zip(DILATIONS, ((qa, ka, va), (qa4, ka4, va4), (qa16, ka16, va16))):
            o, lse = _banded(*(a.reshape(batch * dil, seq // dil, A_WIDTH) for a in qkv))
            shape = (batch * seq, A_WIDTH) if dil == 1 else (batch, dil, seq // dil, A_WIDTH)
            a_outs += [o.reshape(shape), lse.reshape(shape)]

        of, ob = _hgrn(lbl2, qb, zf, zb, ib, batch, seq, l, rows=tm)

        oc = _diff_attn(diff_lambda[l].astype(F32), diff_norm[l].astype(F32).reshape(2 * HEAD_DIM, 1),
                        qct, kc, vct, batch, seq, l, tq=tiles.q_tile, kc=tiles.key_chunk)

        hg = jnp.tile(hgrn_norm[l].astype(F32), N_HEADS).reshape(1, B_WIDTH)
        h2 = _out_proj(x2, a_outs, of, ob, gb, oc, hg, w_out[l].astype(BF16),
                       norm_post_mix[l].reshape(1, d), seq, tm)

        x2 = _ffn(h2, norm_pre_ffn[l].reshape(1, d), w_up[l].astype(BF16), conv_w[l],
                  conv_b[l].reshape(1, -1), w_down[l].astype(BF16), norm_post_ffn[l].reshape(1, d),
                  seq, tm, cn=tiles.ff_chunk)
    return x2.reshape(batch, seq, d)
```

```python
import functools
import math
from typing import NamedTuple

import jax
import jax.numpy as jnp
from jax import lax
from jax.experimental import pallas as pl
from jax.experimental.pallas import tpu as pltpu

F32 = jnp.float32
BF16 = jnp.bfloat16

HEAD_DIM = 64
N_HEADS = 4
A_WIDTH = N_HEADS * HEAD_DIM
B_WIDTH = N_HEADS * HEAD_DIM
C_WIDTH = N_HEADS * 2 * HEAD_DIM
ROPE_THETA = 500000.0
ROPE_DIM = HEAD_DIM // 4
ROPE_HALF = ROPE_DIM // 2
DILATIONS = (1, 4, 16)
BAND_HALF = 64
HGRN_CHUNK = 64
CONV_WIDTH = 3
EPS = 1e-6
NEG_INF = -1e30
LOG2E = math.log2(math.e)
LN2 = math.log(2.0)
EXP2_CLAMP = 115.0

LANES = 128
BF16_ROWS = 16
MXU_TILE = 256
VMEM_LIMIT = 56 * 1024 * 1024

OFF_QA, OFF_KA, OFF_VA = 0, 256, 512
OFF_QB, OFF_ZF, OFF_ZB, OFF_IB, OFF_GB = 768, 1024, 1280, 1536, 1792
OFF_QC, OFF_KC, OFF_VC = 2048, 2560, 3072


def _cparams(sem):
    return pltpu.CompilerParams(dimension_semantics=sem, vmem_limit_bytes=VMEM_LIMIT)


def _rms(x, g):
    ms = jnp.mean(x * x, axis=-1, keepdims=True)
    return x * lax.rsqrt(ms + EPS) * g


def _dot(a, b):
    return jnp.dot(a, b, preferred_element_type=F32)


def _dot_nt(a, b):
    return lax.dot_general(a, b, (((1,), (1,)), ((), ())), preferred_element_type=F32)


def _dot_tn(a, b):
    return lax.dot_general(a, b, (((0,), (0,)), ((), ())), preferred_element_type=F32)


def _in_proj_kernel(x_ref, g_ref, w_ref, rc_ref, rs1_ref, rs2_ref,
                    qa_ref, ka_ref, va_ref, qa4_ref, ka4_ref, va4_ref, qa16_ref, ka16_ref, va16_ref,
                    qb_ref, zf_ref, zb_ref, ib_ref, gb_ref, qct_ref, kc_ref, vct_ref,
                    sq_ref, sk_ref, sv_ref):
    tm = x_ref.shape[0]
    xn = _rms(x_ref[...], g_ref[...]).astype(BF16)

    def emit_a(val, stage_ref, nat_ref, sub_refs):
        nat_ref[...] = val.astype(BF16)
        halves = range(A_WIDTH // LANES)
        for h in halves:
            stage_ref[h] = val[:, h * LANES:(h + 1) * LANES]
        for dil, ref in zip(DILATIONS[1:], sub_refs):
            for r in range(dil):
                ref[0, r, :, :] = jnp.concatenate(
                    [stage_ref[h, pl.ds(r, tm // dil, stride=dil), :] for h in halves], axis=1).astype(BF16)

    rc, rs1, rs2 = rc_ref[...], rs1_ref[...], rs2_ref[...]

    def proj(c0, n):
        return _dot(xn, w_ref[:, c0:c0 + n])

    def rope(a):
        outs = []
        for j in range(a.shape[1] // LANES):
            blk = a[:, j * LANES:(j + 1) * LANES]
            outs.append(blk * rc + pltpu.roll(blk, LANES - ROPE_HALF, 1) * rs1
                        + pltpu.roll(blk, ROPE_HALF, 1) * rs2)
        return outs

    def cat(blks):
        return jnp.concatenate(blks, axis=1)

    scale = 1.0 / math.sqrt(HEAD_DIM)
    for half in range(2):
        qblks = rope(proj(OFF_QC + 256 * half, 256))
        kblks = rope(proj(OFF_KC + 256 * half, 256))
        v = proj(OFF_VC + 256 * half, 256)
        kc_ref[:, 256 * half:256 * (half + 1)] = cat(kblks).astype(BF16)
        for j in range(2):
            r0 = 256 * half + LANES * j
            qct_ref[0, r0:r0 + LANES, :] = (qblks[j] * (scale * LOG2E)).T.astype(BF16)
            vct_ref[0, r0:r0 + LANES, :] = v[:, j * LANES:(j + 1) * LANES].T.astype(BF16)
    emit_a(cat(rope(proj(OFF_QA, 256))) * (scale * LOG2E), sq_ref, qa_ref, (qa4_ref, qa16_ref))
    emit_a(cat(rope(proj(OFF_KA, 256))), sk_ref, ka_ref, (ka4_ref, ka16_ref))
    emit_a(proj(OFF_VA, 256), sv_ref, va_ref, (va4_ref, va16_ref))
    qb_ref[...] = proj(OFF_QB, 256).astype(BF16)
    zf_ref[...] = proj(OFF_ZF, 256)
    zb_ref[...] = proj(OFF_ZB, 256)
    ib_ref[...] = proj(OFF_IB, 256).astype(BF16)
    gb_ref[...] = proj(OFF_GB, 256)


def _in_proj(x2, g, w, rc, rs1, rs2, batch, seq, tm):
    t, d = x2.shape
    n_cols = w.shape[1]
    tiles_per_seq = seq // tm
    row = lambda i: (i, 0)
    const = lambda i: (0, 0)
    ropei = lambda i: (i % tiles_per_seq, 0)
    tr = lambda i: (i // tiles_per_seq, 0, i % tiles_per_seq)
    nat = lambda n, dt: jax.ShapeDtypeStruct((t, n), dt)
    sub_shape = lambda dil: jax.ShapeDtypeStruct((batch, dil, seq // dil, A_WIDTH), BF16)
    sub_spec = lambda dil: pl.BlockSpec((1, dil, tm // dil, A_WIDTH),
                                        lambda i: (i // tiles_per_seq, 0, i % tiles_per_seq, 0))
    out_shape = ([nat(256, BF16)] * 3
                 + [sub_shape(DILATIONS[1])] * 3 + [sub_shape(DILATIONS[2])] * 3
                 + [nat(256, BF16), nat(256, F32), nat(256, F32), nat(256, BF16), nat(256, F32)]
                 + [jax.ShapeDtypeStruct((batch, C_WIDTH, seq), BF16),
                    nat(C_WIDTH, BF16),
                    jax.ShapeDtypeStruct((batch, C_WIDTH, seq), BF16)])
    out_specs = ([pl.BlockSpec((tm, 256), row)] * 3 + [sub_spec(DILATIONS[1])] * 3 + [sub_spec(DILATIONS[2])] * 3
                 + [pl.BlockSpec((tm, 256), row)] * 5
                 + [pl.BlockSpec((1, C_WIDTH, tm), tr), pl.BlockSpec((tm, C_WIDTH), row),
                    pl.BlockSpec((1, C_WIDTH, tm), tr)])
    return pl.pallas_call(
        _in_proj_kernel,
        grid=(t // tm,),
        in_specs=[pl.BlockSpec((tm, d), row), pl.BlockSpec((1, d), const),
                  pl.BlockSpec((d, n_cols), const),
                  pl.BlockSpec((tm, LANES), ropei), pl.BlockSpec((tm, LANES), ropei),
                  pl.BlockSpec((tm, LANES), ropei)],
        out_specs=out_specs,
        out_shape=out_shape,
        scratch_shapes=[pltpu.VMEM((A_WIDTH // LANES, tm, LANES), F32)] * 3,
        compiler_params=_cparams(("parallel",)),
        name="in_proj",
    )(x2, g, w, rc, rs1, rs2)


def _banded_kernel(q_ref, k_ref, v_ref, o_ref, lse_ref, *, seq_len, q_block, sub, k_win):
    i = pl.program_id(1)
    rows = N_HEADS * sub
    lane = lax.broadcasted_iota(jnp.int32, (rows, A_WIDTH), 1)
    row_head = lax.broadcasted_iota(jnp.int32, (rows, A_WIDTH), 0) // sub
    own_lanes = (lane // HEAD_DIM) == row_head
    rel = (lax.broadcasted_iota(jnp.int32, (rows, k_win), 1)
           - lax.broadcasted_iota(jnp.int32, (rows, k_win), 0) % sub)
    head_lane = lax.broadcasted_iota(jnp.int32, (sub, A_WIDTH), 1) // HEAD_DIM
    n_sub = q_block // sub
    q0s = [i * q_block + sb * sub for sb in range(n_sub)]
    kss = [pl.multiple_of(jnp.clip(q0 - BAND_HALF, 0, seq_len - k_win), BAND_HALF) for q0 in q0s]

    def score(sb):
        qblk = q_ref[0, sb * sub:(sb + 1) * sub, :]
        q4 = jnp.concatenate([qblk] * N_HEADS, axis=0)
        q4 = jnp.where(own_lanes, q4, jnp.zeros_like(q4))
        return _dot_nt(q4, k_ref[0, pl.ds(kss[sb], k_win), :])

    def softmax(sb, s):
        shifted = rel + (kss[sb] - q0s[sb] + BAND_HALF)
        band = shifted.astype(jnp.uint32) <= 2 * BAND_HALF
        s = jnp.where(band, s, NEG_INF)
        m = jnp.max(s, axis=-1, keepdims=True)
        p = jnp.exp2(s - m)
        den = jnp.sum(p, axis=-1, keepdims=True)
        return p.astype(BF16), den, (m + jnp.log2(den)) * LN2

    def per_head(x4):
        out = x4[(N_HEADS - 1) * sub:N_HEADS * sub]
        for h in reversed(range(N_HEADS - 1)):
            out = jnp.where(head_lane == h, x4[h * sub:(h + 1) * sub], out)
        return out

    ss = [score(sb) for sb in range(n_sub)]
    sm = [softmax(sb, s) for sb, s in enumerate(ss)]
    pv = [_dot(p, v_ref[0, pl.ds(kss[sb], k_win), :]) for sb, (p, _, _) in enumerate(sm)]
    for sb in range(n_sub):
        _, den, lse = sm[sb]
        o_ref[0, sb * sub:(sb + 1) * sub, :] = per_head(pv[sb]) / per_head(den)
        lse_ref[0, sb * sub:(sb + 1) * sub, :] = per_head(lse)


def _banded(q, k, v):
    n_seq, seq_len, w = q.shape
    q_block = min(4 * MXU_TILE, seq_len)
    sub = min(LANES, seq_len)
    k_win = min(sub + 2 * BAND_HALF, seq_len)
    kern = functools.partial(_banded_kernel, seq_len=seq_len, q_block=q_block, sub=sub, k_win=k_win)
    full = pl.BlockSpec((1, seq_len, w), lambda s, i: (s, 0, 0))
    blk = pl.BlockSpec((1, q_block, w), lambda s, i: (s, i, 0))
    return pl.pallas_call(
        kern,
        grid=(n_seq, seq_len // q_block),
        in_specs=[blk, full, full],
        out_specs=[blk, blk],
        out_shape=[jax.ShapeDtypeStruct((n_seq, seq_len, w), F32)] * 2,
        compiler_params=_cparams(("parallel", "arbitrary")),
        name="banded_attn",
    )(q, k, v)


def _split3(x):
    hi = x.astype(BF16)
    r1 = x - hi.astype(F32)
    mid = r1.astype(BF16)
    lo = (r1 - mid.astype(F32)).astype(BF16)
    return hi, mid, lo


def _hgrn_kernel(lbl_ref, lbd_ref, qf_ref, zf_ref, vf_ref, qb_ref, zb_ref, vb_ref,
                 of_ref, ob_ref, sf_ref, sb_ref, *, depth, layer, n_chunks):
    c = HGRN_CHUNK
    rows = n_chunks * c
    group = lbd_ref.shape[0]

    @pl.when(pl.program_id(2) == 0)
    def _():
        sf_ref[...] = jnp.zeros_like(sf_ref)
        sb_ref[...] = jnp.zeros_like(sb_ref)

    def lower_bound(logits):
        e = jnp.exp(logits - jnp.max(logits, axis=0, keepdims=True))
        p = e / jnp.sum(e, axis=0, keepdims=True)
        lb = jnp.zeros((1, LANES), F32)
        for j in range(1, layer + 1):
            lb = lb + p[j:j + 1, :]
        return lb

    lb_f = lower_bound(lbl_ref[0:depth, :])
    lb_b = lower_bound(lbl_ref[depth:2 * depth, :])

    ti = lax.broadcasted_iota(jnp.int32, (c, c), 0)
    si = lax.broadcasted_iota(jnp.int32, (c, c), 1)
    tril = si <= ti
    triu = si >= ti
    lane_c = lax.broadcasted_iota(jnp.int32, (c, LANES), 1) < HEAD_DIM
    lane_b = lax.broadcasted_iota(jnp.int32, (rows, LANES), 1) < HEAD_DIM
    er = lax.broadcasted_iota(jnp.int32, (LANES, LANES), 0) < HEAD_DIM
    ec = lax.broadcasted_iota(jnp.int32, (LANES, LANES), 1) < HEAD_DIM
    same_head = er == ec
    lbd = lbd_ref[...]

    def cumsum(logf):
        parts = jnp.concatenate(_split3(logf), axis=1)
        outs = []
        for g in range(rows // group):
            r = _dot(lbd, parts[g * group:(g + 1) * group, :])
            outs.append(r[:, 0:LANES] + r[:, LANES:2 * LANES] + r[:, 2 * LANES:3 * LANES])
        return jnp.concatenate(outs, axis=0)

    def chunk_rows(x, off):
        return jnp.concatenate(
            [jnp.broadcast_to(x[j * c + off:j * c + off + 1, :], (c, LANES)) for j in range(n_chunks)], axis=0)

    rs = [slice(j * c, (j + 1) * c) for j in range(n_chunks)]

    def prepare(q_ref, z_ref, v_ref, lb, forward):
        q = q_ref[...].astype(F32)
        z = z_ref[...]
        v = v_ref[...]
        sig = jax.nn.sigmoid(z)
        logf = jnp.log2(lb + (1.0 - lb) * sig)
        kk = (1.0 - lb) * (1.0 - sig)
        a = cumsum(logf)
        last = chunk_rows(a, c - 1)
        if forward:
            e = a
            mid = chunk_rows(a, c // 2 - 1)
            q_in = q * jnp.exp2(a)
            k_st = kk * jnp.exp2(last - a)
            tri = tril
        else:
            e = a - logf
            mid = chunk_rows(e, c // 2)
            q_in = q * jnp.exp2(last - e)
            k_st = kk * jnp.exp2(e)
            tri = triu
        sgn = 1.0 if forward else -1.0
        qd = q * jnp.exp2(jnp.minimum(sgn * (e - mid), EXP2_CLAMP))
        kd = (kk * jnp.exp2(jnp.minimum(sgn * (mid - e), EXP2_CLAMP))).astype(BF16)
        q0 = jnp.where(lane_b, qd, 0.0).astype(BF16)
        q1 = jnp.where(lane_b, 0.0, qd).astype(BF16)
        decay = [jnp.exp2(a[(j + 1) * c - 1:(j + 1) * c, :]) for j in range(n_chunks)]
        return dict(q0=q0, q1=q1, kd=kd, v=v, q_in=q_in.astype(BF16), k_st=k_st.astype(BF16), tri=tri,
                    decay=decay, forward=forward)

    def first_dots(d):
        return ([_dot_nt(d["q0"][r], d["kd"][r]) for r in rs], [_dot_nt(d["q1"][r], d["kd"][r]) for r in rs],
                [_dot_tn(d["v"][r], d["k_st"][r]) for r in rs])

    def intra_dots(d, s0, s1):
        s0 = [jnp.where(d["tri"], s, 0.0).astype(BF16) for s in s0]
        s1 = [jnp.where(d["tri"], s, 0.0).astype(BF16) for s in s1]
        return [jnp.where(lane_c, _dot(x0, d["v"][r]), _dot(x1, d["v"][r])) for x0, x1, r in zip(s0, s1, rs)]

    def scan(d, ut, st_ref):
        st = st_ref[...]
        states = [None] * n_chunks
        for j in (range(n_chunks) if d["forward"] else reversed(range(n_chunks))):
            states[j] = st.astype(BF16)
            st = st * d["decay"][j] + jnp.where(same_head, ut[j], 0.0)
        st_ref[...] = st
        return states

    def finish(d, states, intra, o_ref):
        for j, r in enumerate(rs):
            o_ref[r, :] = _dot_nt(d["q_in"][r], states[j]) + intra[j]

    df = prepare(qf_ref, zf_ref, vf_ref, lb_f, True)
    db = prepare(qb_ref, zb_ref, vb_ref, lb_b, False)
    f0, f1, fu = first_dots(df)
    b0, b1, bu = first_dots(db)
    fi = intra_dots(df, f0, f1)
    bi = intra_dots(db, b0, b1)
    fs = scan(df, fu, sf_ref)
    bs = scan(db, bu, sb_ref)
    finish(df, fs, fi, of_ref)
    finish(db, bs, bi, ob_ref)


def _hgrn(lb_logits2, q, zf, zb, v, batch, seq, layer, rows):
    t = q.shape[0]
    depth = lb_logits2.shape[0] // 2
    n = seq // rows
    fwd = lambda b, p, i: (b * n + i, p)
    bwd = lambda b, p, i: (b * n + (n - 1 - i), p)
    kern = functools.partial(_hgrn_kernel, depth=depth, layer=layer, n_chunks=rows // HGRN_CHUNK)
    spec = lambda im: pl.BlockSpec((rows, LANES), im)
    group = min(256, rows)
    idx = jnp.arange(group)
    lbd = ((idx[None, :] <= idx[:, None])
           & (idx[None, :] // HGRN_CHUNK == idx[:, None] // HGRN_CHUNK)).astype(BF16)
    return pl.pallas_call(
        kern,
        grid=(batch, B_WIDTH // LANES, n),
        in_specs=[pl.BlockSpec((2 * depth, LANES), lambda b, p, i: (0, p)),
                  pl.BlockSpec((group, group), lambda b, p, i: (0, 0)),
                  spec(fwd), spec(fwd), spec(fwd), spec(bwd), spec(bwd), spec(bwd)],
        out_specs=[spec(fwd), spec(bwd)],
        out_shape=[jax.ShapeDtypeStruct((t, B_WIDTH), F32)] * 2,
        scratch_shapes=[pltpu.VMEM((LANES, LANES), F32), pltpu.VMEM((LANES, LANES), F32)],
        compiler_params=_cparams(("parallel", "parallel", "arbitrary")),
        name="hgrn2",
    )(lb_logits2, lbd, q, zf, v, q, zb, v)


def _diff_kernel(lam_ref, g_ref, q0_ref, q1_ref, q2_ref, k_ref, vt_ref, o_ref,
                 s1_ref, s2_ref, p1_ref, p2_ref, st_ref, *, seq, tq, kc, unroll, lam_init):
    lp = lam_ref[...]
    lam = (jnp.exp(jnp.sum(lp[0:1, :] * lp[1:2, :], axis=1, keepdims=True))
           - jnp.exp(jnp.sum(lp[2:3, :] * lp[3:4, :], axis=1, keepdims=True)) + lam_init)
    n = seq // kc
    sub = 8

    def split(qt):
        row = lax.broadcasted_iota(jnp.int32, qt.shape, 0)
        zero = jnp.zeros_like(qt)
        return jnp.where(row < HEAD_DIM, qt, zero), jnp.where(row < HEAD_DIM, zero, qt)

    def scores(k0, q, s_ref, mrun):
        s = _dot(k_ref[pl.ds(k0, kc), :], q)
        s_ref[pl.ds(k0, kc), :] = s
        return jnp.maximum(mrun, jnp.max(s.reshape(kc // sub, sub, tq), axis=0))

    def expo(k0, s_ref, p_ref, m, lrun):
        p = jnp.exp2(s_ref[pl.ds(k0, kc), :] - m)
        p_ref[pl.ds(k0, kc), :] = p.astype(BF16)
        return lrun + jnp.sum(p.reshape(kc // sub, sub, tq), axis=0)

    part = lambda val: jnp.full((sub, tq), val, F32)
    loop = functools.partial(lax.fori_loop, 0, n, unroll=unroll)
    fill_loop = functools.partial(lax.fori_loop, 0, n, unroll=min(4, n))
    rows = lambda a: slice(a * sub, (a + 1) * sub)
    colmax = lambda a: jnp.max(st_ref[rows(a), :], axis=0, keepdims=True)
    colsum = lambda a: jnp.sum(st_ref[rows(a), :], axis=0, keepdims=True)

    @pl.when(pl.program_id(0) == 0)
    def _():
        qa1, qa2 = split(q0_ref[0])

        def fill_s(j, carry):
            k0 = pl.multiple_of(j * kc, kc)
            return scores(k0, qa1, s1_ref, carry[0]), scores(k0, qa2, s2_ref, carry[1])

        mr1, mr2 = fill_loop(fill_s, (part(NEG_INF), part(NEG_INF)))
        ma1 = jnp.max(mr1, axis=0, keepdims=True)
        ma2 = jnp.max(mr2, axis=0, keepdims=True)
        qb1, qb2 = split(q1_ref[0])

        def fill_e(j, carry):
            l1, l2, mr1, mr2 = carry
            k0 = pl.multiple_of(j * kc, kc)
            l1 = expo(k0, s1_ref, p1_ref, ma1, l1)
            mr1 = scores(k0, qb1, s1_ref, mr1)
            l2 = expo(k0, s2_ref, p2_ref, ma2, l2)
            mr2 = scores(k0, qb2, s2_ref, mr2)
            return l1, l2, mr1, mr2

        l1, l2, mr1, mr2 = fill_loop(fill_e, (part(0.0), part(0.0), part(NEG_INF), part(NEG_INF)))
        st_ref[rows(0), :] = mr1
        st_ref[rows(1), :] = mr2
        st_ref[rows(2), :] = l1
        st_ref[rows(3), :] = l2

    m1, m2 = colmax(0), colmax(1)
    l1, l2 = colsum(2), colsum(3)
    c16 = jnp.broadcast_to((lam * l1 / l2).astype(BF16), (BF16_ROWS, tq))
    qs1, qs2 = split(q2_ref[0])

    def body(j, carry):
        acc, e1, e2, mr1, mr2 = carry
        k0 = pl.multiple_of(j * kc, kc)
        p2 = p2_ref[pl.ds(k0, kc), :].reshape(kc // BF16_ROWS, BF16_ROWS, tq)
        w = p1_ref[pl.ds(k0, kc), :] - (p2 * c16).reshape(kc, tq)
        acc = acc + _dot(vt_ref[0, :, pl.ds(k0, kc)], w)
        e1 = expo(k0, s1_ref, p1_ref, m1, e1)
        mr1 = scores(k0, qs1, s1_ref, mr1)
        e2 = expo(k0, s2_ref, p2_ref, m2, e2)
        mr2 = scores(k0, qs2, s2_ref, mr2)
        return acc, e1, e2, mr1, mr2

    acc, e1, e2, mr1, mr2 = loop(body, (jnp.zeros((2 * HEAD_DIM, tq), F32), part(0.0), part(0.0),
                                        part(NEG_INF), part(NEG_INF)))
    st_ref[rows(0), :] = mr1
    st_ref[rows(1), :] = mr2
    st_ref[rows(2), :] = e1
    st_ref[rows(3), :] = e2
    o = acc / l1
    ms = jnp.mean(o * o, axis=0, keepdims=True)
    y = o * lax.rsqrt(ms + EPS) * g_ref[...] * (1.0 - lam_init)
    o_ref[...] = y.T.astype(o_ref.dtype)


def _diff_attn(lam_p, g_col, qct, kc_nat, vct, batch, seq, layer, tq, kc):
    t = kc_nat.shape[0]
    lam_init = 0.8 - 0.6 * math.exp(-0.3 * layer)
    nq = seq // tq
    kern = functools.partial(_diff_kernel, seq=seq, tq=tq, kc=kc, unroll=min(16, seq // kc), lam_init=lam_init)
    total = batch * N_HEADS * nq
    assert nq >= 3, (seq, tq)

    def at(g, ahead):
        gg = jnp.minimum(g + ahead, total - 1)
        return gg // (N_HEADS * nq), (gg // nq) % N_HEADS, gg % nq

    def q_spec(ahead):
        return pl.BlockSpec((1, 2 * HEAD_DIM, tq), lambda g: at(g, ahead))

    def k_map(g):
        b, h, _ = at(g, 2)
        return b, h

    def v_map(g):
        b, h, _ = at(g, 0)
        return b, h, 0

    def o_map(g):
        b, h, i = at(g, 0)
        return b * nq + i, h

    return pl.pallas_call(
        kern,
        grid=(total,),
        in_specs=[pl.BlockSpec(lam_p.shape, lambda g: (0, 0)),
                  pl.BlockSpec((2 * HEAD_DIM, 1), lambda g: (0, 0)),
                  q_spec(0), q_spec(1), q_spec(2),
                  pl.BlockSpec((seq, 2 * HEAD_DIM), k_map),
                  pl.BlockSpec((1, 2 * HEAD_DIM, seq), v_map)],
        out_specs=pl.BlockSpec((tq, 2 * HEAD_DIM), o_map),
        out_shape=jax.ShapeDtypeStruct((t, C_WIDTH), BF16),
        scratch_shapes=[pltpu.VMEM((seq, tq), F32), pltpu.VMEM((seq, tq), F32),
                        pltpu.VMEM((seq, tq), BF16), pltpu.VMEM((seq, tq), BF16),
                        pltpu.VMEM((32, tq), F32)],
        compiler_params=_cparams(("arbitrary",)),
        name="diff_attn",
    )(lam_p, g_col, qct, qct, qct, kc_nat, vct)


def _out_proj_kernel(x_ref, o1_ref, l1_ref, o2_ref, l2_ref, o3_ref, l3_ref,
                     of_ref, ob_ref, gb_ref, oc_ref, hg_ref, w_ref, g_ref, h_ref, *stage_refs):
    tm = x_ref.shape[0]
    ab_width = A_WIDTH + B_WIDTH
    mix_c = _dot(oc_ref[...], w_ref[ab_width:, :])

    def natural(ref, dil, stage_ref):
        halves = range(A_WIDTH // LANES)
        for r in range(dil):
            blk = ref[0, r]
            for h in halves:
                stage_ref[h, pl.ds(r, tm // dil, stride=dil), :] = blk[:, h * LANES:(h + 1) * LANES]
        return jnp.concatenate([stage_ref[h] for h in halves], axis=1)

    o2, l2 = natural(o2_ref, DILATIONS[1], stage_refs[0]), natural(l2_ref, DILATIONS[1], stage_refs[1])
    o3, l3 = natural(o3_ref, DILATIONS[2], stage_refs[2]), natural(l3_ref, DILATIONS[2], stage_refs[3])
    l1 = l1_ref[...]
    mx = jnp.maximum(jnp.maximum(l1, l2), l3)
    w1, w2, w3 = jnp.exp(l1 - mx), jnp.exp(l2 - mx), jnp.exp(l3 - mx)
    oa = (w1 * o1_ref[...] + w2 * o2 + w3 * o3) / (w1 + w2 + w3)

    y = of_ref[...] + ob_ref[...]
    y2 = y * y
    lane = lax.broadcasted_iota(jnp.int32, y.shape, 1)
    ms = jnp.zeros_like(y)
    for h in range(N_HEADS):
        in_head = (lane >= h * HEAD_DIM) & (lane < (h + 1) * HEAD_DIM)
        ms_h = jnp.sum(jnp.where(in_head, y2, 0.0), axis=-1, keepdims=True) * (1.0 / HEAD_DIM)
        ms = jnp.where(in_head, ms_h, ms)
    gb = gb_ref[...]
    ob = y * lax.rsqrt(ms + EPS) * hg_ref[...] * (gb * jax.nn.sigmoid(gb))

    cat = jnp.concatenate([oa.astype(BF16), ob.astype(BF16)], axis=1)
    mix = _dot(cat, w_ref[0:ab_width, :]) + mix_c
    h_ref[...] = x_ref[...] + _rms(mix, g_ref[...])


def _out_proj(x2, a_outs, of, ob, gb, oc, hg, w, g, seq, tm):
    t, d = x2.shape
    tiles_per_seq = seq // tm
    row = lambda i: (i, 0)
    const = lambda i: (0, 0)
    r256 = pl.BlockSpec((tm, 256), row)
    sub = lambda dil: pl.BlockSpec((1, dil, tm // dil, A_WIDTH),
                                   lambda i: (i // tiles_per_seq, 0, i % tiles_per_seq, 0))
    a_specs = [r256, r256] + [sub(DILATIONS[1])] * 2 + [sub(DILATIONS[2])] * 2
    return pl.pallas_call(
        _out_proj_kernel,
        grid=(t // tm,),
        in_specs=[pl.BlockSpec((tm, d), row)] + a_specs + [r256] * 3 + [
            pl.BlockSpec((tm, C_WIDTH), row), pl.BlockSpec((1, B_WIDTH), const),
            pl.BlockSpec(w.shape, const), pl.BlockSpec((1, d), const)],
        out_specs=pl.BlockSpec((tm, d), row),
        out_shape=jax.ShapeDtypeStruct((t, d), F32),
        scratch_shapes=[pltpu.VMEM((A_WIDTH // LANES, tm, LANES), F32)] * 4,
        compiler_params=_cparams(("parallel",)),
        name="out_proj",
    )(x2, *a_outs, of, ob, gb, oc, hg, w, g)


def _ffn_kernel(hp_ref, h_ref, hn_ref, gpre_ref, wup_ref, cw_ref, cb_ref, wdn_ref, gpost_ref,
                o_ref, xe_ref, act_ref, *, tm, tiles_per_seq, d_ff, cn):
    i = pl.program_id(0)
    halo = BF16_ROWS
    first = (i % tiles_per_seq) == 0
    last = (i % tiles_per_seq) == tiles_per_seq - 1
    gpre = gpre_ref[...]
    h = h_ref[...]
    xe_ref[halo:halo + tm, :] = _rms(h, gpre).astype(BF16)
    xe_ref[0:halo, :] = jnp.where(first, 0.0, _rms(hp_ref[...], gpre)).astype(BF16)
    xe_ref[halo + tm:2 * halo + tm, :] = jnp.where(last, 0.0, _rms(hn_ref[...], gpre)).astype(BF16)
    xe = xe_ref[...]

    def conv(c0):
        u = _dot(xe, wup_ref[:, c0:c0 + cn])
        cw = cw_ref[:, c0:c0 + cn]
        return (u[halo - 1:halo - 1 + tm] * cw[0:1] + u[halo:halo + tm] * cw[1:2]
                + u[halo + 1:halo + 1 + tm] * cw[2:3] + cb_ref[:, c0:c0 + cn])

    for c in range(d_ff // cn):
        gate = conv(c * cn)
        val = conv(d_ff + c * cn)
        act_ref[:, c * cn:(c + 1) * cn] = (gate * jax.nn.sigmoid(gate) * val).astype(BF16)
    ff = _dot(act_ref[...], wdn_ref[...])
    o_ref[...] = h + _rms(ff, gpost_ref[...])


def _ffn(h2, gpre, wup, cw, cb, wdn, gpost, seq, tm, cn):
    t, d = h2.shape
    d_ff = wdn.shape[0]
    halo = BF16_ROWS
    tiles_per_seq = seq // tm
    hb = tm // halo
    const = lambda i: (0, 0)
    kern = functools.partial(_ffn_kernel, tm=tm, tiles_per_seq=tiles_per_seq, d_ff=d_ff, cn=cn)
    return pl.pallas_call(
        kern,
        grid=(t // tm,),
        in_specs=[pl.BlockSpec((halo, d), lambda i: (jnp.maximum(i * hb - 1, 0), 0)),
                  pl.BlockSpec((tm, d), lambda i: (i, 0)),
                  pl.BlockSpec((halo, d), lambda i: (jnp.minimum((i + 1) * hb, t // halo - 1), 0)),
                  pl.BlockSpec((1, d), const),
                  pl.BlockSpec(wup.shape, const), pl.BlockSpec(cw.shape, const),
                  pl.BlockSpec(cb.shape, const), pl.BlockSpec(wdn.shape, const),
                  pl.BlockSpec((1, d), const)],
        out_specs=pl.BlockSpec((tm, d), lambda i: (i, 0)),
        out_shape=jax.ShapeDtypeStruct((t, d), F32),
        scratch_shapes=[pltpu.VMEM((tm + 2 * halo, d), BF16), pltpu.VMEM((tm, d_ff), BF16)],
        compiler_params=_cparams(("parallel",)),
        name="conv_ffn",
    )(h2, h2, h2, gpre, wup, cw, cb, wdn, gpost)


def _rope_tables(seq):
    pos = jnp.arange(seq, dtype=F32)
    inv = ROPE_THETA ** (-jnp.arange(0, ROPE_DIM, 2, dtype=F32) / ROPE_DIM)
    ang = pos[:, None] * inv[None, :]
    cos, sin = jnp.cos(ang), jnp.sin(ang)
    rest = HEAD_DIM - ROPE_DIM
    one, zero = jnp.ones((seq, rest), F32), jnp.zeros((seq, rest), F32)
    zh = jnp.zeros((seq, ROPE_HALF), F32)
    rep = LANES // HEAD_DIM
    rc = jnp.tile(jnp.concatenate([cos, cos, one], axis=1), (1, rep))
    rs1 = jnp.tile(jnp.concatenate([-sin, zh, zero], axis=1), (1, rep))
    rs2 = jnp.tile(jnp.concatenate([zh, sin, zero], axis=1), (1, rep))
    return rc, rs1, rs2


class _Tiles(NamedTuple):
    rows: int
    q_tile: int
    key_chunk: int
    ff_chunk: int


def _tiles(seq):
    assert seq % (DILATIONS[-1] * BF16_ROWS) == 0 and seq % HGRN_CHUNK == 0, seq
    rows = min(2 * MXU_TILE, seq)
    assert seq % rows == 0 and rows % (DILATIONS[-1] * BF16_ROWS) == 0, (seq, rows)
    return _Tiles(rows=rows, q_tile=min(MXU_TILE, seq), key_chunk=min(2 * MXU_TILE, seq), ff_chunk=MXU_TILE)


def kernel(x, w_in, w_out, lb_logits, hgrn_norm, diff_lambda, diff_norm, w_up, conv_w, conv_b,
           w_down, norm_pre_mix, norm_post_mix, norm_pre_ffn, norm_post_ffn):
    batch, seq, d = x.shape
    depth = w_in.shape[0]
    tiles = _tiles(seq)
    tm = tiles.rows
    rc, rs1, rs2 = _rope_tables(seq)
    lbl2 = lb_logits.astype(F32).reshape(2 * depth, B_WIDTH)
    x2 = x.reshape(batch * seq, d)
    for l in range(depth):
        (qa, ka, va, qa4, ka4, va4, qa16, ka16, va16, qb, zf, zb, ib, gb, qct, kc, vct) = _in_proj(
            x2, norm_pre_mix[l].reshape(1, d), w_in[l].astype(BF16), rc, rs1, rs2, batch, seq, tm)

        a_outs = []
        for dil, qkv in zip(DILATIONS, ((qa, ka, va), (qa4, ka4, va4), (qa16, ka16, va16))):
            o, lse = _banded(*(a.reshape(batch * dil, seq // dil, A_WIDTH) for a in qkv))
            shape = (batch * seq, A_WIDTH) if dil == 1 else (batch, dil, seq // dil, A_WIDTH)
            a_outs += [o.reshape(shape), lse.reshape(shape)]

        of, ob = _hgrn(lbl2, qb, zf, zb, ib, batch, seq, l, rows=tm)

        oc = _diff_attn(diff_lambda[l].astype(F32), diff_norm[l].astype(F32).reshape(2 * HEAD_DIM, 1),
                        qct, kc, vct, batch, seq, l, tq=tiles.q_tile, kc=tiles.key_chunk)

        hg = jnp.tile(hgrn_norm[l].astype(F32), N_HEADS).reshape(1, B_WIDTH)
        h2 = _out_proj(x2, a_outs, of, ob, gb, oc, hg, w_out[l].astype(BF16),
                       norm_post_mix[l].reshape(1, d), seq, tm)

        x2 = _ffn(h2, norm_pre_ffn[l].reshape(1, d), w_up[l].astype(BF16), conv_w[l],
                  conv_b[l].reshape(1, -1), w_down[l].astype(BF16), norm_post_ffn[l].reshape(1, d),
                  seq, tm, cn=tiles.ff_chunk)
    return x2.reshape(batch, seq, d)
```
